```python
import jax
import jax.numpy as jnp
from jax import lax
import numpy as np

D_MODEL = 2048
BATCH = 16
SEQ = 2048
DEPTH = 1
DEC_BATCH = 32
DEC_SEQ = 8
PAST_LEN = 16384
PAGE_SIZE = 128

A_HEADS = 8
A_KDIM = 128
A_VDIM = 128
A_CHUNK = 16
B_HEADS = 8
B_HDIM = 128
MOBA_BLOCK = 256
MOBA_TOPK = 3
MOBA_QBLOCK = 128
ROPE_THETA = 500000.0
ROT_DIM = B_HDIM // 4
N_EXPERTS = 32
TOP_K = 4
D_FF = D_MODEL
SWIGLU_LIMIT = 7.0
SWIGLU_ALPHA = 1.702
MOE_BLOCK = 256
PLE_DIM = 256
RMS_EPS = 1e-6
NEG_BIG = -1e30

A_KW = A_HEADS * A_KDIM
A_VW = A_HEADS * A_VDIM
B_W = B_HEADS * B_HDIM
IN_WIDTHS = (A_KW, A_KW, A_VW, A_VW, B_W, B_W, B_W, D_MODEL, D_MODEL)
IN_DIM = sum(IN_WIDTHS)

kernel_name = 'hgrn2_moba_gated_parallel_moe_step'


def rmsnorm(x, w):
    xf = x.astype(jnp.float32)
    y = xf * lax.rsqrt(jnp.mean(xf * xf, axis=-1, keepdims=True) + RMS_EPS)
    return (y * w.astype(jnp.float32)).astype(x.dtype)


def split_cols(u):
    outs, start = [], 0
    for w in IN_WIDTHS:
        outs.append(u[..., start:start + w])
        start += w
    return outs


def partial_rotary(x, pos):
    half = ROT_DIM // 2
    inv_freq = jnp.power(ROPE_THETA, -jnp.arange(half, dtype=jnp.float32) * (2.0 / ROT_DIM))
    ang = pos.astype(jnp.float32)[:, None] * inv_freq[None, :]
    cos = jnp.cos(ang)[:, None, :]
    sin = jnp.sin(ang)[:, None, :]
    xr = x[..., :ROT_DIM].astype(jnp.float32)
    x1, x2 = xr[..., :half], xr[..., half:]
    rot = jnp.concatenate([x1 * cos - x2 * sin, x2 * cos + x1 * sin], axis=-1).astype(x.dtype)
    return jnp.concatenate([rot, x[..., ROT_DIM:]], axis=-1)


def hgrn2_recurrence(q, k, v, logf, s0):
    bsz, L = q.shape[0], q.shape[1]
    C = A_CHUNK
    n_chunks = -(-L // C)
    pad = n_chunks * C - L

    def to_chunks(a):
        a = jnp.pad(a, ((0, 0), (0, pad), (0, 0), (0, 0)))
        return a.reshape(bsz, n_chunks, C, a.shape[2], a.shape[3]).transpose(1, 0, 3, 2, 4)

    tri = jnp.tril(jnp.ones((C, C), dtype=bool))[:, :, None]

    def step(S, xs):
        qc, kc, vc, gc = xs
        b = jnp.cumsum(gc, axis=2)
        diff = b[:, :, :, None, :] - b[:, :, None, :, :]
        decay = jnp.where(tri, jnp.exp(jnp.where(tri, diff, 0.0)), 0.0)
        scores = jnp.einsum('bhtsk,bhsk->bhts', qc[:, :, :, None, :] * decay, kc)
        o = (jnp.einsum('bhts,bhsv->bhtv', scores, vc)
             + jnp.einsum('bhtk,bhkv->bhtv', qc * jnp.exp(b), S))
        b_end = b[:, :, -1:, :]
        S_new = (jnp.exp(b_end[:, :, 0, :])[..., None] * S
                 + jnp.einsum('bhsk,bhsv->bhkv', kc * jnp.exp(b_end - b), vc))
        return S_new, o

    S_fin, o = lax.scan(step, s0, (to_chunks(q), to_chunks(k), to_chunks(v), to_chunks(logf)))
    o = o.transpose(1, 0, 3, 2, 4).reshape(bsz, n_chunks * C, q.shape[2], v.shape[3])[:, :L]
    return o, S_fin


def hgrn2_branch(uq, uf, ui, ug, lb, norm_w, s0):
    bsz, L = uq.shape[0], uq.shape[1]
    shp_k = (bsz, L, A_HEADS, A_KDIM)
    shp_v = (bsz, L, A_HEADS, A_VDIM)
    q = jax.nn.silu(uq.astype(jnp.float32)).reshape(shp_k) * (A_KDIM ** -0.5)
    f = (lb + (1.0 - lb) * jax.nn.sigmoid(uf.astype(jnp.float32))).reshape(shp_k)
    k = 1.0 - f
    v = ui.astype(jnp.float32).reshape(shp_v)
    o, s_new = hgrn2_recurrence(q, k, v, jnp.log(f), s0.astype(jnp.float32))
    g = ug.astype(jnp.float32).reshape(shp_v)
    o = o * lax.rsqrt(jnp.mean(o * o, axis=-1, keepdims=True) + RMS_EPS) * norm_w.astype(jnp.float32) * jax.nn.silu(g)
    return o.reshape(bsz, L, A_VW).astype(uq.dtype), s_new


def moba_prompt(q, k, v):
    bsz, L = q.shape[0], q.shape[1]
    n_blk = -(-L // MOBA_BLOCK)
    pad = n_blk * MOBA_BLOCK - L
    n_cand = n_blk - 1
    k_sel = min(MOBA_TOPK, n_cand)
    n_qb = L // MOBA_QBLOCK
    scale = B_HDIM ** -0.5
    qh = q.transpose(0, 2, 1, 3)

    def blocks(a):
        a = jnp.pad(a.transpose(0, 2, 1, 3), ((0, 0), (0, 0), (0, pad), (0, 0)))
        return a.reshape(bsz, B_HEADS, n_blk, MOBA_BLOCK, B_HDIM)

    kb, vb = blocks(k), blocks(v)
    kmean = jnp.mean(kb[:, :, :n_cand].astype(jnp.float32), axis=3)
    head_ix = jnp.arange(B_HEADS)[:, None, None]

    def one(args):
        bi, qi = args
        q0 = qi * MOBA_QBLOCK
        qs = lax.dynamic_slice_in_dim(qh[bi], q0, MOBA_QBLOCK, axis=1)
        qpos = q0 + jnp.arange(MOBA_QBLOCK)
        own = q0 // MOBA_BLOCK
        k_b, v_b = kb[bi], vb[bi]
        k_own = lax.dynamic_index_in_dim(k_b, own, axis=1, keepdims=False)
        v_own = lax.dynamic_index_in_dim(v_b, own, axis=1, keepdims=False)
        kpos = own * MOBA_BLOCK + jnp.arange(MOBA_BLOCK)
        s_own = jnp.einsum('hqd,hkd->hqk', qs, k_own).astype(jnp.float32) * scale
        s_own = jnp.where(kpos[None, None, :] <= qpos[None, :, None], s_own, NEG_BIG)
        if k_sel == 0:
            p = jax.nn.softmax(s_own, axis=-1).astype(v.dtype)
            return jnp.einsum('hqk,hkd->hqd', p, v_own)
        gate = jnp.einsum('hqd,hnd->hqn', qs.astype(jnp.float32), kmean[bi])
        gate = jnp.where(jnp.arange(n_cand) < own, gate, NEG_BIG)
        _, idx = lax.top_k(gate, k_sel)
        k_rows = k_b[head_ix, idx].reshape(B_HEADS, MOBA_QBLOCK, k_sel * MOBA_BLOCK, B_HDIM)
        v_rows = v_b[head_ix, idx].reshape(B_HEADS, MOBA_QBLOCK, k_sel * MOBA_BLOCK, B_HDIM)
        s_sel = jnp.einsum('hqd,hqnd->hqn', qs, k_rows).astype(jnp.float32) * scale
        slot_ok = jnp.repeat(jnp.arange(k_sel) < own, MOBA_BLOCK)
        s_sel = jnp.where(slot_ok, s_sel, NEG_BIG)
        p = jax.nn.softmax(jnp.concatenate([s_own, s_sel], axis=-1), axis=-1).astype(v.dtype)
        return (jnp.einsum('hqk,hkd->hqd', p[..., :MOBA_BLOCK], v_own)
                + jnp.einsum('hqn,hqnd->hqd', p[..., MOBA_BLOCK:], v_rows))

    b_ids = jnp.repeat(jnp.arange(bsz), n_qb)
    q_ids = jnp.tile(jnp.arange(n_qb), bsz)
    out = lax.map(one, (b_ids, q_ids))
    out = out.reshape(bsz, n_qb, B_HEADS, MOBA_QBLOCK, B_HDIM).transpose(0, 1, 3, 2, 4)
    return out.reshape(bsz, L, B_W)


def moba_sample(q, k, v, cache_k, cache_v, page_table, layer):
    dbsz, T = q.shape[0], q.shape[1]
    own = PAST_LEN // MOBA_BLOCK
    own_cached = PAST_LEN - own * MOBA_BLOCK
    ppb = MOBA_BLOCK // PAGE_SIZE
    n_cand = own
    k_sel = min(MOBA_TOPK, n_cand)
    scale = B_HDIM ** -0.5
    qpos = PAST_LEN + jnp.arange(T)
    kpos = own * MOBA_BLOCK + jnp.arange(own_cached + T)
    causal = kpos[None, None, :] <= qpos[None, :, None]
    head_ix = jnp.arange(B_HEADS)[:, None, None, None]

    def rows(pool, pages):
        return pool[layer, pages].reshape(-1, B_HEADS, B_HDIM).transpose(1, 0, 2)

    def one(args):
        qs, kn, vn, pt = args
        qs, kn, vn = qs.transpose(1, 0, 2), kn.transpose(1, 0, 2), vn.transpose(1, 0, 2)
        if own_cached > 0:
            own_pages = pt[own * ppb: own * ppb + own_cached // PAGE_SIZE]
            k_own = jnp.concatenate([rows(cache_k, own_pages), kn], axis=1)
            v_own = jnp.concatenate([rows(cache_v, own_pages), vn], axis=1)
        else:
            k_own, v_own = kn, vn
        s_own = jnp.einsum('hqd,hkd->hqk', qs, k_own).astype(jnp.float32) * scale
        s_own = jnp.where(causal, s_own, NEG_BIG)
        if k_sel == 0:
            p = jax.nn.softmax(s_own, axis=-1).astype(v.dtype)
            return jnp.einsum('hqk,hkd->hqd', p, v_own)
        kmean = cache_k[layer, pt[:n_cand * ppb]].astype(jnp.float32)
        kmean = kmean.reshape(n_cand, MOBA_BLOCK, B_HEADS, B_HDIM).mean(axis=1)
        gate = jnp.einsum('hqd,nhd->hqn', qs.astype(jnp.float32), kmean)
        _, idx = lax.top_k(gate, k_sel)
        phys = pt[idx[..., None] * ppb + jnp.arange(ppb)]
        k_rows = cache_k[layer, phys, :, head_ix, :].reshape(B_HEADS, T, k_sel * MOBA_BLOCK, B_HDIM)
        v_rows = cache_v[layer, phys, :, head_ix, :].reshape(B_HEADS, T, k_sel * MOBA_BLOCK, B_HDIM)
        s_sel = jnp.einsum('hqd,hqnd->hqn', qs, k_rows).astype(jnp.float32) * scale
        L_own = own_cached + T
        p = jax.nn.softmax(jnp.concatenate([s_own, s_sel], axis=-1), axis=-1).astype(v.dtype)
        return (jnp.einsum('hqk,hkd->hqd', p[..., :L_own], v_own)
                + jnp.einsum('hqn,hqnd->hqd', p[..., L_own:], v_rows))

    out = lax.map(one, (q, k, v, page_table))
    return out.transpose(0, 2, 1, 3).reshape(dbsz, T, B_W)


def moe_ffn(h, w_router, b_router, w1, b1, w2, b2):
    n_tok = h.shape[0]
    logits = (h @ w_router + b_router).astype(jnp.float32)
    top_val, top_idx = lax.top_k(logits, TOP_K)
    gates = jax.nn.softmax(top_val, axis=-1)
    n_asg = n_tok * TOP_K
    e_flat = top_idx.reshape(-1)
    tok_flat = jnp.repeat(jnp.arange(n_tok, dtype=jnp.int32), TOP_K)
    g_flat = gates.reshape(-1)
    order = jnp.argsort(e_flat)
    e_s, tok_s, g_s = e_flat[order], tok_flat[order], g_flat[order]
    counts = jnp.zeros((N_EXPERTS,), jnp.int32).at[e_flat].add(1)
    starts = jnp.cumsum(counts) - counts
    pcounts = (counts + MOE_BLOCK - 1) // MOE_BLOCK * MOE_BLOCK
    pends = jnp.cumsum(pcounts)
    pstarts = pends - pcounts
    dest = pstarts[e_s] + jnp.arange(n_asg, dtype=jnp.int32) - starts[e_s]
    n_blocks = -(-(n_asg + N_EXPERTS * (MOE_BLOCK - 1)) // MOE_BLOCK)
    n_rows = n_blocks * MOE_BLOCK
    row_tok = jnp.full((n_rows,), n_tok, jnp.int32).at[dest].set(tok_s)
    row_gate = jnp.zeros((n_rows,), jnp.float32).at[dest].set(g_s)
    blk_exp = jnp.minimum(jnp.searchsorted(pends, jnp.arange(n_blocks, dtype=jnp.int32) * MOE_BLOCK, side='right'), N_EXPERTS - 1)
    h_pad = jnp.concatenate([h, jnp.zeros((1, h.shape[1]), h.dtype)], axis=0)

    def expert_block(args):
        e, toks = args
        xb = h_pad[toks]
        gu = xb @ w1[e] + b1[e]
        glu = jnp.minimum(gu[:, :D_FF], SWIGLU_LIMIT)
        lin = jnp.clip(gu[:, D_FF:], -SWIGLU_LIMIT, SWIGLU_LIMIT)
        act = glu * jax.nn.sigmoid(SWIGLU_ALPHA * glu) * (lin + 1.0)
        return act @ w2[e] + b2[e]

    y_rows = lax.map(expert_block, (blk_exp, row_tok.reshape(n_blocks, MOE_BLOCK)))
    y = jnp.zeros((n_tok + 1, h.shape[1]), jnp.float32).at[row_tok].add(
        y_rows.reshape(n_rows, -1).astype(jnp.float32) * row_gate[:, None])
    return y[:n_tok].astype(h.dtype)


def decoder_layer(x, p, pos, s0, attend, lb, norm_mix, w_in, hgrn_norm, w_a, w_b, w_o,
                  norm_moe, w_router, b_router, w1, b1, w2, b2, norm_ple, w_ple, w_ple_gate):
    bsz, L, _ = x.shape
    h = rmsnorm(x, norm_mix)
    uq, uf, ui, ug, bq, bk, bv, ga, gb = split_cols(h @ w_in)
    o_a, s_new = hgrn2_branch(uq, uf, ui, ug, lb, hgrn_norm, s0)
    hs = (bsz, L, B_HEADS, B_HDIM)
    q = partial_rotary(bq.reshape(hs), pos)
    k = partial_rotary(bk.reshape(hs), pos)
    v = bv.reshape(hs)
    o_b = attend(q, k, v)
    merged = jax.nn.sigmoid(ga) * (o_a @ w_a) + jax.nn.sigmoid(gb) * (o_b @ w_b)
    x = x + merged @ w_o
    h2 = rmsnorm(x, norm_moe)
    x = x + moe_ffn(h2.reshape(bsz * L, -1), w_router, b_router, w1, b1, w2, b2).reshape(bsz, L, -1)
    x = x + jax.nn.sigmoid(rmsnorm(x, norm_ple) @ w_ple_gate) * (p @ w_ple)
    return x, s_new.astype(x.dtype), k, v


def setup_inputs(seed: int = 0) -> dict:
    key = jax.random.key(seed)
    ks = jax.random.split(key, 32)
    f32 = jnp.float32
    n_pages = PAST_LEN // PAGE_SIZE
    n_used = DEC_BATCH * n_pages
    n_pool = n_used + (n_used + 3) // 4

    def nrm(k, shape, scale):
        return jax.random.normal(k, shape, f32) * scale

    page_table = jax.random.permutation(ks[0], n_pool)[:n_used].reshape(DEC_BATCH, n_pages).astype(jnp.int32)
    return {
        'x_prompt': nrm(ks[1], (BATCH, SEQ, D_MODEL), 1.0),
        'x_sample': nrm(ks[2], (DEC_BATCH, DEC_SEQ, D_MODEL), 1.0),
        'cache_k': nrm(ks[3], (DEPTH, n_pool, PAGE_SIZE, B_HEADS, B_HDIM), 1.0),
        'cache_v': nrm(ks[4], (DEPTH, n_pool, PAGE_SIZE, B_HEADS, B_HDIM), 1.0),
        'state_hgrn': nrm(ks[5], (DEPTH, DEC_BATCH, A_HEADS, A_KDIM, A_VDIM), 1.0),
        'page_table': page_table,
        'p_prompt': nrm(ks[6], (DEPTH, BATCH, SEQ, PLE_DIM), 1.0),
        'p_sample': nrm(ks[7], (DEPTH, DEC_BATCH, DEC_SEQ, PLE_DIM), 1.0),
        'hgrn_lb': nrm(ks[8], (DEPTH + 1, A_KW), 1.0),
        'norm_mix': 1.0 + nrm(ks[9], (DEPTH, D_MODEL), 0.05),
        'w_in': nrm(ks[10], (DEPTH, D_MODEL, IN_DIM), D_MODEL ** -0.5),
        'hgrn_norm': 1.0 + nrm(ks[11], (DEPTH, A_VDIM), 0.05),
        'w_a': nrm(ks[12], (DEPTH, A_VW, D_MODEL), A_VW ** -0.5),
        'w_b': nrm(ks[13], (DEPTH, B_W, D_MODEL), B_W ** -0.5),
        'w_o': nrm(ks[14], (DEPTH, D_MODEL, D_MODEL), D_MODEL ** -0.5),
        'norm_moe': 1.0 + nrm(ks[15], (DEPTH, D_MODEL), 0.05),
        'w_router': nrm(ks[16], (DEPTH, D_MODEL, N_EXPERTS), D_MODEL ** -0.5),
        'b_router': nrm(ks[17], (DEPTH, N_EXPERTS), 0.01),
        'w_moe1': nrm(ks[18], (DEPTH, N_EXPERTS, D_MODEL, 2 * D_FF), D_MODEL ** -0.5),
        'b_moe1': nrm(ks[19], (DEPTH, N_EXPERTS, 2 * D_FF), 0.02),
        'w_moe2': nrm(ks[20], (DEPTH, N_EXPERTS, D_FF, D_MODEL), D_FF ** -0.5),
        'b_moe2': nrm(ks[21], (DEPTH, N_EXPERTS, D_MODEL), 0.02),
        'norm_ple': 1.0 + nrm(ks[22], (DEPTH, D_MODEL), 0.05),
        'w_ple': nrm(ks[23], (DEPTH, PLE_DIM, D_MODEL), PLE_DIM ** -0.5),
        'w_ple_gate': nrm(ks[24], (DEPTH, D_MODEL, D_MODEL), D_MODEL ** -0.5),
        'norm_final': 1.0 + nrm(ks[25], (D_MODEL,), 0.05),
    }


def reference(x_prompt, x_sample, cache_k, cache_v, state_hgrn, page_table, p_prompt, p_sample,
              hgrn_lb, norm_mix, w_in, hgrn_norm, w_a, w_b, w_o, norm_moe, w_router, b_router,
              w_moe1, b_moe1, w_moe2, b_moe2, norm_ple, w_ple, w_ple_gate, norm_final):
    lb_all = jnp.cumsum(jax.nn.softmax(hgrn_lb.astype(jnp.float32), axis=0), axis=0)
    pos_p = jnp.arange(x_prompt.shape[1])
    pos_s = PAST_LEN + jnp.arange(x_sample.shape[1])
    s0_p = jnp.zeros((x_prompt.shape[0], A_HEADS, A_KDIM, A_VDIM), jnp.float32)
    xp, xs = x_prompt, x_sample
    sp_l, kp_l, vp_l, ss_l, ks_l, vs_l = [], [], [], [], [], []
    for l in range(DEPTH):
        lw = (lb_all[l], norm_mix[l], w_in[l], hgrn_norm[l], w_a[l], w_b[l], w_o[l], norm_moe[l],
              w_router[l], b_router[l], w_moe1[l], b_moe1[l], w_moe2[l], b_moe2[l],
              norm_ple[l], w_ple[l], w_ple_gate[l])
        xp, sp, kp, vp = decoder_layer(xp, p_prompt[l], pos_p, s0_p, moba_prompt, *lw)
        attend_s = lambda q, k, v, layer=l: moba_sample(q, k, v, cache_k, cache_v, page_table, layer)
        xs, ss, ks, vs = decoder_layer(xs, p_sample[l], pos_s, state_hgrn[l], attend_s, *lw)
        sp_l.append(sp)
        kp_l.append(kp)
        vp_l.append(vp)
        ss_l.append(ss)
        ks_l.append(ks)
        vs_l.append(vs)
    y_prompt = rmsnorm(xp, norm_final)
    y_sample = rmsnorm(xs, norm_final)
    return (y_prompt, y_sample, jnp.stack(sp_l), jnp.stack(kp_l), jnp.stack(vp_l),
            jnp.stack(ss_l), jnp.stack(ks_l), jnp.stack(vs_l))
```

```python
import functools

import numpy as np
import jax
import jax.numpy as jnp
from jax import lax
from jax.experimental import pallas as pl
from jax.experimental.pallas import tpu as pltpu

D_MODEL = 2048
DEPTH = 1
PAST_LEN = 16384
PAGE_SIZE = 128
A_HEADS = 8
A_KDIM = 128
A_VDIM = 128
B_HEADS = 8
B_HDIM = 128
MOBA_BLOCK = 256
MOBA_TOPK = 3
MOBA_QBLOCK = 128
ROPE_THETA = 500000.0
ROT_DIM = B_HDIM // 4
N_EXPERTS = 32
TOP_K = 4
D_FF = D_MODEL
SWIGLU_LIMIT = 7.0
SWIGLU_ALPHA = 1.702
PLE_DIM = 256
RMS_EPS = 1e-6
NEG_BIG = -1e30

A_KW = A_HEADS * A_KDIM
A_VW = A_HEADS * A_VDIM
B_W = B_HEADS * B_HDIM
IN_WIDTHS = (A_KW, A_KW, A_VW, A_VW, B_W, B_W, B_W, D_MODEL, D_MODEL)
IN_DIM = sum(IN_WIDTHS)
IN_OFFS = tuple(int(v) for v in np.cumsum((0,) + IN_WIDTHS[:-1]))

LANES = 128
SUBLANES = 8
VMEM_LIMIT = 56 * 1024 * 1024

ROW_TILE = 256
MM_ROWS = 1024
MM_COLS = 1024
HGRN_CHUNK = 128
MOE_ROWS = 512
MOE_FF = 512
SEL_PAGES = 8

F32 = jnp.float32
BF16 = jnp.bfloat16
HIGHEST = lax.Precision.HIGHEST


def _cparams(*sem):
    return pltpu.CompilerParams(dimension_semantics=sem, vmem_limit_bytes=VMEM_LIMIT)


def _dot(a, b, **kw):
    return jnp.dot(a, b, preferred_element_type=F32, **kw)


def _dot_nt(a, b, **kw):
    return lax.dot_general(a, b, (((1,), (1,)), ((), ())), preferred_element_type=F32, **kw)


def _dot_tn(a, b, **kw):
    return lax.dot_general(a, b, (((0,), (0,)), ((), ())), preferred_element_type=F32, **kw)


def _sigmoid(x):
    return 1.0 / (1.0 + jnp.exp(-x))


def _rms(x, w):
    ms = jnp.mean(x * x, axis=-1, keepdims=True)
    return x * lax.rsqrt(ms + RMS_EPS) * w


def _rmsnorm_body(x_ref, w_ref, o_ref):
    o_ref[...] = _rms(x_ref[...], w_ref[...]).astype(o_ref.dtype)


def rmsnorm_cast(x, w, bm):
    n, d = x.shape
    return pl.pallas_call(
        _rmsnorm_body,
        grid=(n // bm,),
        in_specs=[pl.BlockSpec((bm, d), lambda i: (i, 0)),
                  pl.BlockSpec((1, d), lambda i: (0, 0))],
        out_specs=pl.BlockSpec((bm, d), lambda i: (i, 0)),
        out_shape=jax.ShapeDtypeStruct((n, d), BF16),
        compiler_params=_cparams("parallel"),
        name="rmsnorm_cast",
    )(x, w.reshape(1, d))


def _matmul_body(a_ref, w_ref, o_ref):
    o_ref[...] = _dot(a_ref[...], w_ref[...]).astype(o_ref.dtype)


def matmul(a, w, bm, bn):
    n, k = a.shape
    m = w.shape[1]
    return pl.pallas_call(
        _matmul_body,
        grid=(m // bn, n // bm),
        in_specs=[pl.BlockSpec((bm, k), lambda j, i: (i, 0)),
                  pl.BlockSpec((k, bn), lambda j, i: (0, j))],
        out_specs=pl.BlockSpec((bm, bn), lambda j, i: (i, j)),
        out_shape=jax.ShapeDtypeStruct((n, m), F32),
        compiler_params=_cparams("parallel", "parallel"),
        name="in_proj",
    )(a, w)


def _rope_tables(pos):
    half = ROT_DIM // 2
    inv_freq = jnp.power(ROPE_THETA, -jnp.arange(half, dtype=F32) * (2.0 / ROT_DIM))
    ang = pos.astype(F32)[:, None] * inv_freq[None, :]
    cos, sin = jnp.cos(ang), jnp.sin(ang)
    n = pos.shape[0]
    ones = jnp.ones((n, B_HDIM - ROT_DIM), F32)
    zeros = jnp.zeros((n, B_HDIM - ROT_DIM), F32)
    zh = jnp.zeros((n, half), F32)
    c = jnp.concatenate([cos, cos, ones], axis=1)
    s_up = jnp.concatenate([-sin, zh, zeros], axis=1)
    s_dn = jnp.concatenate([zh, sin, zeros], axis=1)
    return c, s_up, s_dn


def _rope_body(q_ref, k_ref, v_ref, c_ref, su_ref, sd_ref, qo_ref, ko_ref, vo_ref):
    c, su, sd = c_ref[...], su_ref[...], sd_ref[...]
    half = ROT_DIM // 2
    for h in range(B_HEADS):
        sl = slice(h * B_HDIM, (h + 1) * B_HDIM)
        for src, dst in ((q_ref, qo_ref), (k_ref, ko_ref)):
            x = src[:, sl]
            y = x * c + pltpu.roll(x, B_HDIM - half, 1) * su + pltpu.roll(x, half, 1) * sd
            dst[:, sl] = y
    vo_ref[...] = v_ref[...]


def rope_qkv(u, tabs, bm):
    n = u.shape[0]
    c, su, sd = tabs
    tb = c.shape[0] // bm
    cq, ck, cv = (IN_OFFS[4] // B_W, IN_OFFS[5] // B_W, IN_OFFS[6] // B_W)
    tab_spec = pl.BlockSpec((bm, B_HDIM), lambda i: (i % tb, 0))
    out_spec = pl.BlockSpec((bm, B_W), lambda i: (i, 0))
    out_sds = jax.ShapeDtypeStruct((n, B_W), F32)
    return pl.pallas_call(
        _rope_body,
        grid=(n // bm,),
        in_specs=[pl.BlockSpec((bm, B_W), lambda i: (i, cq)),
                  pl.BlockSpec((bm, B_W), lambda i: (i, ck)),
                  pl.BlockSpec((bm, B_W), lambda i: (i, cv)),
                  tab_spec, tab_spec, tab_spec],
        out_specs=(out_spec, out_spec, out_spec),
        out_shape=(out_sds, out_sds, out_sds),
        compiler_params=_cparams("parallel"),
        name="rope_qkv",
    )(u, u, u, c, su, sd)


def _level_matrix(c):
    t = np.arange(c)[:, None]
    s = np.arange(c)[None, :]
    x = np.bitwise_xor(t, s)
    lvl = np.where(x > 0, np.floor(np.log2(np.maximum(x, 1))).astype(np.int32) + 1, 0)
    return np.where(s > t, -1, lvl).astype(np.int32)


def _hgrn_body(uq_ref, uf_ref, ui_ref, ug_ref, lb_ref, nw_ref, s0_ref, lvl_ref,
               o_ref, sout_ref, st_ref, *, rows, chunk, nchunks):
    ci = pl.program_id(2)

    @pl.when(ci == 0)
    def _():
        st_ref[...] = s0_ref[...].T

    def padded(x, fill):
        if rows == chunk:
            return x
        return jnp.concatenate([x, jnp.full((chunk - rows, x.shape[1]), fill, x.dtype)], axis=0)

    lb = lb_ref[...]
    uq = uq_ref[...]
    q = padded(uq * _sigmoid(uq) * (A_KDIM ** -0.5), 0.0)
    f = padded(lb + (1.0 - lb) * _sigmoid(uf_ref[...]), 1.0)
    kk = 1.0 - f
    g = jnp.log(f)
    v = padded(ui_ref[...], 0.0)
    vb = v.astype(BF16)

    rowi = lax.broadcasted_iota(jnp.int32, (chunk, chunk), 0)
    coli = lax.broadcasted_iota(jnp.int32, (chunk, chunk), 1)
    tri = jnp.where(rowi >= coli, 1.0, 0.0).astype(F32)
    b = _dot(tri, g, precision=HIGHEST)

    lvl = lvl_ref[...]
    scores = jnp.where(lvl == 0, _dot_nt(q.astype(BF16), kk.astype(BF16)), 0.0)
    rsub = lax.broadcasted_iota(jnp.int32, (chunk, A_KDIM), 0)
    x = b
    blk, level = 1, 1
    while blk < chunk:
        nxt = pltpu.roll(x, chunk - blk, 0)
        qt = (q * jnp.exp(jnp.minimum(b - x, 0.0))).astype(BF16)
        kt = (kk * jnp.exp(jnp.minimum(nxt - b, 0.0))).astype(BF16)
        scores = jnp.where(lvl == level, _dot_nt(qt, kt), scores)
        x = jnp.where((rsub & blk) != 0, pltpu.roll(x, blk, 0), x)
        blk *= 2
        level += 1

    st = st_ref[...]
    o = _dot(scores.astype(BF16), vb) + _dot_nt((q * jnp.exp(b)).astype(BF16), st.astype(BF16))
    b_end = b[chunk - 1:chunk, :]
    kd = (kk * jnp.exp(b_end - b)).astype(BF16)
    st_new = st * jnp.exp(b_end) + _dot_tn(vb, kd)
    st_ref[...] = st_new

    o = o[:rows]
    ug = ug_ref[...]
    o = o * lax.rsqrt(jnp.mean(o * o, axis=-1, keepdims=True) + RMS_EPS) * nw_ref[...] * (ug * _sigmoid(ug))
    o_ref[...] = o.astype(o_ref.dtype)

    @pl.when(ci == nchunks - 1)
    def _():
        sout_ref[...] = st_new.T


def hgrn2(u, lb, norm_w, s0, bsz, seq, out_dtype):
    assert A_KDIM == LANES and A_VDIM == LANES
    rows = min(seq, HGRN_CHUNK)
    assert seq % rows == 0
    chunk = max(rows, 2 * SUBLANES)
    nchunks = seq // rows
    lvl = jnp.asarray(_level_matrix(chunk))
    h = A_HEADS

    def u_spec(group):
        off = IN_OFFS[group] // LANES
        return pl.BlockSpec((rows, LANES), lambda b, hh, c: (b * nchunks + c, off + hh))

    vec_spec = pl.BlockSpec((1, LANES), lambda b, hh, c: (0, hh))
    state_spec = pl.BlockSpec((None, None, A_KDIM, A_VDIM), lambda b, hh, c: (b, hh, 0, 0))
    return pl.pallas_call(
        functools.partial(_hgrn_body, rows=rows, chunk=chunk, nchunks=nchunks),
        grid=(bsz, h, nchunks),
        in_specs=[u_spec(0), u_spec(1), u_spec(2), u_spec(3), vec_spec,
                  pl.BlockSpec((1, LANES), lambda b, hh, c: (0, 0)),
                  state_spec,
                  pl.BlockSpec((chunk, chunk), lambda b, hh, c: (0, 0))],
        out_specs=(pl.BlockSpec((rows, LANES), lambda b, hh, c: (b * nchunks + c, hh)), state_spec),
        out_shape=(jax.ShapeDtypeStruct((bsz * seq, A_VW), out_dtype),
                   jax.ShapeDtypeStruct((bsz, h, A_KDIM, A_VDIM), F32)),
        scratch_shapes=[pltpu.VMEM((A_VDIM, A_KDIM), F32)],
        compiler_params=_cparams("parallel", "parallel", "arbitrary"),
        name="hgrn2",
    )(u, u, u, u, lb.reshape(1, A_KW), norm_w.reshape(1, A_VDIM), s0, lvl)


def _moba_prompt_body(q_ref, k_ref, v_ref, e_ref, o_ref, kb_ref, vb_ref, km_ref, *, seq, nblk):
    kf = k_ref[...]
    kb_ref[...] = kf.astype(BF16)
    vb_ref[...] = v_ref[...].astype(BF16)
    km_ref[...] = jnp.zeros_like(km_ref)
    km_ref[0:nblk, :] = jnp.sum(kf.reshape(nblk, MOBA_BLOCK, B_HDIM), axis=1) * (1.0 / MOBA_BLOCK)

    qb = MOBA_QBLOCK
    n_cand = nblk - 1
    k_sel = min(MOBA_TOPK, n_cand)
    scale = B_HDIM ** -0.5
    lane = lax.broadcasted_iota(jnp.int32, (qb, LANES), 1)
    col = lax.broadcasted_iota(jnp.int32, (qb, seq), 1)
    rowq = lax.broadcasted_iota(jnp.int32, (qb, seq), 0)

    def qblock(qi, carry):
        q0 = pl.multiple_of(qi * qb, qb)
        qs = q_ref[pl.ds(q0, qb), :]
        own = q0 // MOBA_BLOCK
        own_keys = (col >= own * MOBA_BLOCK) & (col <= q0 + rowq)
        if k_sel > 0:
            gate = _dot_nt(qs, km_ref[...], precision=HIGHEST)
            gm = jnp.where(lane < own, gate, NEG_BIG)
            rank = jnp.zeros((qb, LANES), jnp.int32)
            for m in range(n_cand):
                cm = gm[:, m:m + 1]
                ahead = (cm > gm) | ((cm == gm) & (lane > m))
                rank = rank + jnp.where(ahead, 1, 0)
            sel = (rank < k_sel) & (lane < own)
            expand = _dot(jnp.where(sel, 1.0, 0.0).astype(BF16), e_ref[...])
            allowed = (expand > 0.5) | own_keys
        else:
            allowed = own_keys
        s = _dot_nt(qs.astype(BF16), kb_ref[...]) * scale
        s = jnp.where(allowed, s, NEG_BIG)
        m = jnp.max(s, axis=-1, keepdims=True)
        p = jnp.exp(s - m)
        l = jnp.sum(p, axis=-1, keepdims=True)
        o = _dot(p.astype(BF16), vb_ref[...]) / l
        o_ref[pl.ds(q0, qb), :] = o.astype(o_ref.dtype)
        return carry

    lax.fori_loop(0, seq // qb, qblock, 0)


def moba_prompt(q, k, v, bsz, seq):
    assert seq % MOBA_BLOCK == 0 and B_HDIM == LANES and seq // MOBA_BLOCK <= LANES
    nblk = seq // MOBA_BLOCK
    e = np.zeros((LANES, seq), np.float32)
    for n in range(nblk):
        e[n, n * MOBA_BLOCK:(n + 1) * MOBA_BLOCK] = 1.0
    spec = pl.BlockSpec((seq, B_HDIM), lambda b, h: (b, h))
    return pl.pallas_call(
        functools.partial(_moba_prompt_body, seq=seq, nblk=nblk),
        grid=(bsz, B_HEADS),
        in_specs=[spec, spec, spec, pl.BlockSpec((LANES, seq), lambda b, h: (0, 0))],
        out_specs=spec,
        out_shape=jax.ShapeDtypeStruct((bsz * seq, B_W), BF16),
        scratch_shapes=[pltpu.VMEM((seq, B_HDIM), BF16), pltpu.VMEM((seq, B_HDIM), BF16),
                        pltpu.VMEM((LANES, B_HDIM), F32)],
        compiler_params=_cparams("parallel", "parallel"),
        name="moba_prompt",
    )(q, k, v, jnp.asarray(e, BF16))


def _moba_select_body(pt_ref, *refs, tq, nblk, npages_step):
    page_refs = refs[:npages_step]
    q_ref, o_ref, km_ref = refs[npages_step:]
    j = pl.program_id(1)
    per_block = MOBA_BLOCK // PAGE_SIZE
    for i in range(0, npages_step, per_block):
        acc = jnp.sum(page_refs[i][...], axis=0, keepdims=True)
        for p in range(1, per_block):
            acc = acc + jnp.sum(page_refs[i + p][...], axis=0, keepdims=True)
        km_ref[pl.ds(j * (npages_step // per_block) + i // per_block, 1), :] = acc * (1.0 / MOBA_BLOCK)

    @pl.when(j == pl.num_programs(1) - 1)
    def _():
        k_sel = min(MOBA_TOPK, nblk)
        lane_g = lax.broadcasted_iota(jnp.int32, (tq, nblk), 1)
        lane_o = lax.broadcasted_iota(jnp.int32, (tq, LANES), 1)
        for h in range(B_HEADS):
            sl = slice(h * B_HDIM, (h + 1) * B_HDIM)
            g = _dot_nt(q_ref[:, sl], km_ref[:, sl], precision=HIGHEST)
            res = jnp.zeros((tq, LANES), jnp.int32)
            for r in range(k_sel):
                mx = jnp.max(g, axis=-1, keepdims=True)
                ix = jnp.min(jnp.where(g == mx, lane_g, nblk), axis=-1, keepdims=True)
                res = jnp.where(lane_o == r, ix, res)
                g = jnp.where(lane_g == ix, -jnp.inf, g)
            o_ref[h * tq:(h + 1) * tq, :] = res


def moba_select(q, cache_k, page_table, tq):
    dbsz, npages = page_table.shape
    nblk = (PAST_LEN // MOBA_BLOCK)
    assert PAST_LEN % MOBA_BLOCK == 0 and npages * PAGE_SIZE == PAST_LEN
    ps = min(SEL_PAGES, npages)
    assert npages % ps == 0 and ps % (MOBA_BLOCK // PAGE_SIZE) == 0
    n_pool = cache_k.shape[0]
    ck = cache_k.reshape(n_pool, PAGE_SIZE, B_W)

    def page_spec(i):
        return pl.BlockSpec((None, PAGE_SIZE, B_W), lambda b, j, pt: (pt[b, j * ps + i], 0, 0))

    grid_spec = pltpu.PrefetchScalarGridSpec(
        num_scalar_prefetch=1,
        grid=(dbsz, npages // ps),
        in_specs=[page_spec(i) for i in range(ps)]
        + [pl.BlockSpec((tq, B_W), lambda b, j, pt: (b, 0))],
        out_specs=pl.BlockSpec((None, B_HEADS * tq, LANES), lambda b, j, pt: (b, 0, 0)),
        scratch_shapes=[pltpu.VMEM((nblk, B_W), F32)],
    )
    return pl.pallas_call(
        functools.partial(_moba_select_body, tq=tq, nblk=nblk, npages_step=ps),
        grid_spec=grid_spec,
        out_shape=jax.ShapeDtypeStruct((dbsz, B_HEADS * tq, LANES), jnp.int32),
        compiler_params=_cparams("parallel", "arbitrary"),
        name="moba_select",
    )(page_table, *([ck] * ps), q)


def _moba_decode_body(pt_ref, ix_ref, q_ref, kn_ref, vn_ref, ck_ref, cv_ref, o_ref,
                      kbuf, vbuf, sems, *, tq, k_sel):
    b = pl.program_id(0)
    h = pl.program_id(1)
    per_block = MOBA_BLOCK // PAGE_SIZE
    nsel = tq * k_sel * MOBA_BLOCK
    own_rows = kbuf.shape[0] - nsel

    def copies():
        out = []
        for t in range(tq):
            for s in range(k_sel):
                blk = ix_ref[((b * B_HEADS + h) * tq + t) * k_sel + s]
                for p in range(per_block):
                    page = pt_ref[b, blk * per_block + p]
                    dst = pl.ds(((t * k_sel + s) * per_block + p) * PAGE_SIZE, PAGE_SIZE)
                    out.append(pltpu.make_async_copy(ck_ref.at[page, :, h, :], kbuf.at[dst, :], sems.at[0]))
                    out.append(pltpu.make_async_copy(cv_ref.at[page, :, h, :], vbuf.at[dst, :], sems.at[1]))
        return out

    cps = copies()
    for cp in cps:
        cp.start()
    zpad = jnp.zeros((own_rows - tq, B_HDIM), F32)
    kbuf[pl.ds(nsel, own_rows), :] = jnp.concatenate([kn_ref[...], zpad], axis=0)
    vbuf[pl.ds(nsel, own_rows), :] = jnp.concatenate([vn_ref[...], zpad], axis=0)
    for cp in cps:
        cp.wait()

    ncol = kbuf.shape[0]
    qb = q_ref[...].astype(BF16)
    s = _dot_nt(qb, kbuf[...].astype(BF16)) * (B_HDIM ** -0.5)
    col = lax.broadcasted_iota(jnp.int32, (tq, ncol), 1)
    row = lax.broadcasted_iota(jnp.int32, (tq, ncol), 0)
    lo = row * (k_sel * MOBA_BLOCK)
    allowed = ((col >= lo) & (col < lo + k_sel * MOBA_BLOCK)) | ((col >= nsel) & (col <= nsel + row))
    s = jnp.where(allowed, s, NEG_BIG)
    m = jnp.max(s, axis=-1, keepdims=True)
    p = jnp.exp(s - m)
    l = jnp.sum(p, axis=-1, keepdims=True)
    o_ref[...] = _dot(p.astype(BF16), vbuf[...].astype(BF16)) / l


def moba_decode(q, kn, vn, cache_k, cache_v, page_table, sel, tq):
    dbsz = page_table.shape[0]
    k_sel = min(MOBA_TOPK, PAST_LEN // MOBA_BLOCK)
    assert PAST_LEN % MOBA_BLOCK == 0, "cached rows of the own block are not supported"
    assert k_sel > 0 and tq <= LANES
    nrows = tq * k_sel * MOBA_BLOCK + LANES
    spec = pl.BlockSpec((tq, B_HDIM), lambda b, h, pt, ix: (b, h))
    grid_spec = pltpu.PrefetchScalarGridSpec(
        num_scalar_prefetch=2,
        grid=(dbsz, B_HEADS),
        in_specs=[spec, spec, spec,
                  pl.BlockSpec(memory_space=pl.ANY), pl.BlockSpec(memory_space=pl.ANY)],
        out_specs=spec,
        scratch_shapes=[pltpu.VMEM((nrows, B_HDIM), F32), pltpu.VMEM((nrows, B_HDIM), F32),
                        pltpu.SemaphoreType.DMA((2,))],
    )
    return pl.pallas_call(
        functools.partial(_moba_decode_body, tq=tq, k_sel=k_sel),
        grid_spec=grid_spec,
        out_shape=jax.ShapeDtypeStruct((dbsz * tq, B_W), F32),
        compiler_params=_cparams("arbitrary", "arbitrary"),
        name="moba_decode",
    )(page_table, sel, q, kn, vn, cache_k, cache_v)


def _merge_body(oa_ref, ob_ref, ga_ref, gb_ref, wa_ref, wb_ref, o_ref):
    a = _dot(oa_ref[...].astype(BF16), wa_ref[...])
    b = _dot(ob_ref[...].astype(BF16), wb_ref[...])
    o_ref[...] = (_sigmoid(ga_ref[...]) * a + _sigmoid(gb_ref[...]) * b).astype(o_ref.dtype)


def merge_branches(o_a, o_b, u, w_a, w_b, bm):
    n = o_a.shape[0]
    bn = min(MM_COLS, D_MODEL)
    assert IN_OFFS[7] % bn == 0 and IN_OFFS[8] % bn == 0 and D_MODEL % bn == 0
    ca, cb = IN_OFFS[7] // bn, IN_OFFS[8] // bn
    return pl.pallas_call(
        _merge_body,
        grid=(D_MODEL // bn, n // bm),
        in_specs=[pl.BlockSpec((bm, A_VW), lambda j, i: (i, 0)),
                  pl.BlockSpec((bm, B_W), lambda j, i: (i, 0)),
                  pl.BlockSpec((bm, bn), lambda j, i: (i, ca + j)),
                  pl.BlockSpec((bm, bn), lambda j, i: (i, cb + j)),
                  pl.BlockSpec((A_VW, bn), lambda j, i: (0, j)),
                  pl.BlockSpec((B_W, bn), lambda j, i: (0, j))],
        out_specs=pl.BlockSpec((bm, bn), lambda j, i: (i, j)),
        out_shape=jax.ShapeDtypeStruct((n, D_MODEL), BF16),
        compiler_params=_cparams("parallel", "parallel"),
        name="merge_branches",
    )(o_a, o_b, u, u, w_a, w_b)


def _pack_words(h):
    half = h.shape[1] // 2
    lo = pltpu.bitcast(h[:, :half].astype(BF16).astype(F32), jnp.uint32)
    hi = pltpu.bitcast(h[:, half:].astype(BF16).astype(F32), jnp.uint32)
    return (lo >> 16) | (hi & jnp.uint32(0xFFFF0000))


def _oproj_body(mp_ref, ms_ref, xp_ref, xs_ref, wo_ref, nw_ref, wr_ref, br_ref,
                x1_ref, hp_ref, idx_ref, gate_ref, *, n_prompt_tiles):
    i = pl.program_id(0)
    wpr = D_MODEL // 2 // LANES

    def run(m_ref, x_ref):
        x1 = x_ref[...] + _dot(m_ref[...], wo_ref[...])
        x1_ref[...] = x1
        h2 = _rms(x1, nw_ref[...])
        words = _pack_words(h2)
        for s in range(wpr):
            hp_ref[pl.ds(s, x1.shape[0], stride=wpr), :] = words[:, s * LANES:(s + 1) * LANES]
        logits = _dot(h2, wr_ref[...], precision=HIGHEST) + br_ref[...]
        lane = lax.broadcasted_iota(jnp.int32, logits.shape, 1)
        g = jnp.where(lane < N_EXPERTS, logits, -jnp.inf)
        idx = jnp.zeros(logits.shape, jnp.int32)
        val = jnp.zeros(logits.shape, F32)
        top = None
        for r in range(TOP_K):
            mx = jnp.max(g, axis=-1, keepdims=True)
            ix = jnp.min(jnp.where(g == mx, lane, LANES), axis=-1, keepdims=True)
            top = mx if top is None else top
            idx = jnp.where(lane == r, ix, idx)
            val = jnp.where(lane == r, jnp.exp(mx - top), val)
            g = jnp.where(lane == ix, -jnp.inf, g)
        idx_ref[...] = idx
        gate_ref[...] = val / jnp.sum(val, axis=-1, keepdims=True)

    @pl.when(i < n_prompt_tiles)
    def _():
        run(mp_ref, xp_ref)

    @pl.when(i >= n_prompt_tiles)
    def _():
        run(ms_ref, xs_ref)


def out_proj_router(m_p, m_s, x_p, x_s, w_o, norm_w, w_router, b_router):
    bm = ROW_TILE
    n_p, n_s = x_p.shape[0], x_s.shape[0]
    assert n_p % bm == 0 and n_s % bm == 0 and N_EXPERTS <= LANES and TOP_K <= LANES
    tp, ts = n_p // bm, n_s // bm
    n = n_p + n_s
    wpr = D_MODEL // 2 // LANES
    wr = jnp.zeros((D_MODEL, LANES), F32).at[:, :N_EXPERTS].set(w_router)
    br = jnp.zeros((1, LANES), F32).at[0, :N_EXPERTS].set(b_router)
    p_map = lambda i: (jnp.minimum(i, tp - 1), 0)
    s_map = lambda i: (jnp.maximum(i - tp, 0), 0)
    full = lambda i: (0, 0)
    row = lambda i: (i, 0)
    return pl.pallas_call(
        functools.partial(_oproj_body, n_prompt_tiles=tp),
        grid=(tp + ts,),
        in_specs=[pl.BlockSpec((bm, D_MODEL), p_map), pl.BlockSpec((bm, D_MODEL), s_map),
                  pl.BlockSpec((bm, D_MODEL), p_map), pl.BlockSpec((bm, D_MODEL), s_map),
                  pl.BlockSpec((D_MODEL, D_MODEL), full), pl.BlockSpec((1, D_MODEL), full),
                  pl.BlockSpec((D_MODEL, LANES), full), pl.BlockSpec((1, LANES), full)],
        out_specs=(pl.BlockSpec((bm, D_MODEL), row), pl.BlockSpec((bm * wpr, LANES), row),
                   pl.BlockSpec((bm, LANES), row), pl.BlockSpec((bm, LANES), row)),
        out_shape=(jax.ShapeDtypeStruct((n, D_MODEL), F32),
                   jax.ShapeDtypeStruct((n * wpr, LANES), jnp.uint32),
                   jax.ShapeDtypeStruct((n, LANES), jnp.int32),
                   jax.ShapeDtypeStruct((n, LANES), F32)),
        compiler_params=_cparams("parallel"),
        name="out_proj_router",
    )(m_p, m_s, x_p, x_s, w_o, norm_w.reshape(1, D_MODEL), wr, br)


def _rank_body(idx_ref, rank_ref, cnt_ref, carry_ref):
    i = pl.program_id(0)

    @pl.when(i == 0)
    def _():
        carry_ref[...] = jnp.zeros_like(carry_ref)

    idx = idx_ref[...]
    bt = idx.shape[0]
    lane = lax.broadcasted_iota(jnp.int32, (bt, LANES), 1)
    rowi = lax.broadcasted_iota(jnp.int32, (bt, bt), 0)
    coli = lax.broadcasted_iota(jnp.int32, (bt, bt), 1)
    before = jnp.where(rowi > coli, 1.0, 0.0).astype(BF16)
    base = carry_ref[...]
    res = jnp.zeros((bt, LANES), jnp.int32)
    for k in range(TOP_K):
        onehot = jnp.where(lane == idx[:, k:k + 1], 1.0, 0.0)
        earlier = _dot(before, onehot.astype(BF16))
        rk = jnp.sum(onehot * (earlier + base), axis=-1, keepdims=True)
        res = jnp.where(lane == k, rk.astype(jnp.int32), res)
        base = base + jnp.sum(onehot, axis=0, keepdims=True)
    rank_ref[...] = res
    carry_ref[...] = base
    cnt_ref[...] = jnp.broadcast_to(base, cnt_ref.shape).astype(jnp.int32)


def moe_rank(idx):
    n = idx.shape[0]
    bt = ROW_TILE
    return pl.pallas_call(
        _rank_body,
        grid=(n // bt,),
        in_specs=[pl.BlockSpec((bt, LANES), lambda i: (i, 0))],
        out_specs=(pl.BlockSpec((bt, LANES), lambda i: (i, 0)),
                   pl.BlockSpec((SUBLANES, LANES), lambda i: (0, 0))),
        out_shape=(jax.ShapeDtypeStruct((n, LANES), jnp.int32),
                   jax.ShapeDtypeStruct((SUBLANES, LANES), jnp.int32)),
        scratch_shapes=[pltpu.VMEM((1, LANES), F32)],
        compiler_params=_cparams("arbitrary"),
        name="moe_rank",
    )(idx)


def _dispatch_body(hp_ref, dest_ref, xs_in_ref, xs_ref, dsm, sem_idx, sem, *, bt, wpr):
    del xs_in_ref
    i = pl.program_id(0)
    cp = pltpu.make_async_copy(dest_ref.at[i], dsm, sem_idx)
    cp.start()
    cp.wait()

    def row_copy(r, k):
        d = dsm[r * TOP_K + k]
        return pltpu.make_async_copy(hp_ref.at[pl.ds(pl.multiple_of(r * wpr, wpr), wpr), :],
                                     xs_ref.at[pl.ds(pl.multiple_of(d * wpr, wpr), wpr), :], sem)

    def issue(r, c):
        for k in range(TOP_K):
            row_copy(r, k).start()
        return c

    lax.fori_loop(0, bt, issue, 0)

    def drain(r, c):
        for k in range(TOP_K):
            row_copy(r, k).wait()
        return c

    lax.fori_loop(0, bt, drain, 0)


def moe_dispatch(hp, dest, xs_init):
    bt = ROW_TILE
    wpr = D_MODEL // 2 // LANES
    nt = dest.shape[0]
    return pl.pallas_call(
        functools.partial(_dispatch_body, bt=bt, wpr=wpr),
        grid=(nt,),
        in_specs=[pl.BlockSpec((bt * wpr, LANES), lambda i: (i, 0)),
                  pl.BlockSpec(memory_space=pl.ANY),
                  pl.BlockSpec(memory_space=pl.ANY)],
        out_specs=pl.BlockSpec(memory_space=pl.ANY),
        out_shape=jax.ShapeDtypeStruct(xs_init.shape, xs_init.dtype),
        scratch_shapes=[pltpu.SMEM((bt * TOP_K,), jnp.int32), pltpu.SemaphoreType.DMA,
                        pltpu.SemaphoreType.DMA],
        input_output_aliases={2: 0},
        compiler_params=_cparams("arbitrary"),
        name="moe_dispatch",
    )(hp, dest, xs_init)


def _experts_body(be_ref, nu_ref, x_ref, w1g_ref, w1l_ref, b1g_ref, b1l_ref, w2_ref, b2_ref,
                  o_ref, xb_ref, acc_ref, *, rows, wpr, opr):
    i = pl.program_id(0)
    c = pl.program_id(1)

    @pl.when(i < nu_ref[0])
    def _():
        @pl.when(c == 0)
        def _():
            half = wpr * LANES
            for s in range(wpr):
                w = x_ref[pl.ds(s, rows, stride=wpr), :]
                lo = pltpu.bitcast(w << 16, F32)
                hi = pltpu.bitcast(w & jnp.uint32(0xFFFF0000), F32)
                xb_ref[:, s * LANES:(s + 1) * LANES] = lo.astype(BF16)
                xb_ref[:, half + s * LANES:half + (s + 1) * LANES] = hi.astype(BF16)
            acc_ref[...] = jnp.broadcast_to(b2_ref[...], acc_ref.shape)

        xb = xb_ref[...]
        gu = _dot(xb, w1g_ref[...]) + b1g_ref[...]
        li = _dot(xb, w1l_ref[...]) + b1l_ref[...]
        glu = jnp.minimum(gu, SWIGLU_LIMIT)
        lin = jnp.clip(li, -SWIGLU_LIMIT, SWIGLU_LIMIT)
        act = glu * _sigmoid(SWIGLU_ALPHA * glu) * (lin + 1.0)
        acc_ref[...] += _dot(act.astype(BF16), w2_ref[...])

        @pl.when(c == pl.num_programs(1) - 1)
        def _():
            acc = acc_ref[...]
            for s in range(opr):
                o_ref[pl.ds(s, rows, stride=opr), :] = acc[:, s * LANES:(s + 1) * LANES]


def moe_experts(xs, blk_exp, n_used, w1, b1, w2, b2):
    rows, ff = MOE_ROWS, min(MOE_FF, D_FF)
    wpr = D_MODEL // 2 // LANES
    opr = D_MODEL // LANES
    n_rows = xs.shape[0] // wpr
    nb = n_rows // rows
    nc = D_FF // ff
    lin_off = D_FF // ff

    def eff(i, c, be, nu):
        live = i < nu[0]
        return jnp.where(live, i, nu[0] - 1), jnp.where(live, c, nc - 1)

    def x_map(i, c, be, nu):
        return (eff(i, c, be, nu)[0], 0)

    def w1g_map(i, c, be, nu):
        ii, cc = eff(i, c, be, nu)
        return (be[ii], 0, cc)

    def w1l_map(i, c, be, nu):
        ii, cc = eff(i, c, be, nu)
        return (be[ii], 0, lin_off + cc)

    def w2_map(i, c, be, nu):
        ii, cc = eff(i, c, be, nu)
        return (be[ii], cc, 0)

    def b2_map(i, c, be, nu):
        return (be[eff(i, c, be, nu)[0]], 0, 0)

    grid_spec = pltpu.PrefetchScalarGridSpec(
        num_scalar_prefetch=2,
        grid=(nb, nc),
        in_specs=[pl.BlockSpec((rows * wpr, LANES), x_map),
                  pl.BlockSpec((None, D_MODEL, ff), w1g_map),
                  pl.BlockSpec((None, D_MODEL, ff), w1l_map),
                  pl.BlockSpec((None, 1, ff), w1g_map),
                  pl.BlockSpec((None, 1, ff), w1l_map),
                  pl.BlockSpec((None, ff, D_MODEL), w2_map),
                  pl.BlockSpec((None, 1, D_MODEL), b2_map)],
        out_specs=pl.BlockSpec((rows * opr, LANES), lambda i, c, be, nu: (i, 0)),
        scratch_shapes=[pltpu.VMEM((rows, D_MODEL), BF16), pltpu.VMEM((rows, D_MODEL), F32)],
    )
    return pl.pallas_call(
        functools.partial(_experts_body, rows=rows, wpr=wpr, opr=opr),
        grid_spec=grid_spec,
        out_shape=jax.ShapeDtypeStruct((n_rows * opr, LANES), F32),
        compiler_params=_cparams("arbitrary", "arbitrary"),
        name="moe_experts",
    )(blk_exp, n_used, xs, w1, w1, b1, b1, w2, b2)


def _combine_body(x1_ref, gate_ref, dest_ref, y_ref, o_ref, buf, dsm, sem_idx, sem, *, bt, opr):
    i = pl.program_id(0)
    cp = pltpu.make_async_copy(dest_ref.at[i], dsm, sem_idx)
    cp.start()
    cp.wait()

    def row_copy(r, k):
        d = dsm[r * TOP_K + k]
        return pltpu.make_async_copy(y_ref.at[pl.ds(pl.multiple_of(d * opr, opr), opr), :],
                                     buf.at[k, pl.ds(pl.multiple_of(r * opr, opr), opr), :], sem)

    def issue(r, c):
        for k in range(TOP_K):
            row_copy(r, k).start()
        return c

    lax.fori_loop(0, bt, issue, 0)

    def drain(r, c):
        for k in range(TOP_K):
            row_copy(r, k).wait()
        return c

    lax.fori_loop(0, bt, drain, 0)

    gates = gate_ref[...]
    for s in range(opr):
        sl = slice(s * LANES, (s + 1) * LANES)
        acc = x1_ref[:, sl]
        for k in range(TOP_K):
            acc = acc + gates[:, k:k + 1] * buf[k, pl.ds(s, bt, stride=opr), :]
        o_ref[:, sl] = acc


def moe_combine(x1, gates, dest, y):
    bt = ROW_TILE
    opr = D_MODEL // LANES
    n = x1.shape[0]
    return pl.pallas_call(
        functools.partial(_combine_body, bt=bt, opr=opr),
        grid=(n // bt,),
        in_specs=[pl.BlockSpec((bt, D_MODEL), lambda i: (i, 0)),
                  pl.BlockSpec((bt, LANES), lambda i: (i, 0)),
                  pl.BlockSpec(memory_space=pl.ANY),
                  pl.BlockSpec(memory_space=pl.ANY)],
        out_specs=pl.BlockSpec((bt, D_MODEL), lambda i: (i, 0)),
        out_shape=jax.ShapeDtypeStruct((n, D_MODEL), F32),
        scratch_shapes=[pltpu.VMEM((TOP_K, bt * opr, LANES), F32),
                        pltpu.SMEM((bt * TOP_K,), jnp.int32),
                        pltpu.SemaphoreType.DMA, pltpu.SemaphoreType.DMA],
        compiler_params=_cparams("arbitrary"),
        name="moe_combine",
    )(x1, gates, dest, y)


def _final_body(x_ref, p_ref, nw_ref, wg_ref, wp_ref, nf_ref, o_ref):
    x = x_ref[...]
    gate = _sigmoid(_dot(_rms(x, nw_ref[...]).astype(BF16), wg_ref[...]))
    x = x + gate * _dot(p_ref[...].astype(BF16), wp_ref[...])
    o_ref[...] = _rms(x, nf_ref[...])


def final_stage(x2_all, row_off, p, norm_ple, w_gate, w_ple, norm_final, bm):
    n = p.shape[0]
    off = row_off // bm
    assert row_off % bm == 0 and n % bm == 0
    full = lambda i: (0, 0)
    return pl.pallas_call(
        _final_body,
        grid=(n // bm,),
        in_specs=[pl.BlockSpec((bm, D_MODEL), lambda i: (off + i, 0)),
                  pl.BlockSpec((bm, PLE_DIM), lambda i: (i, 0)),
                  pl.BlockSpec((1, D_MODEL), full), pl.BlockSpec((D_MODEL, D_MODEL), full),
                  pl.BlockSpec((PLE_DIM, D_MODEL), full), pl.BlockSpec((1, D_MODEL), full)],
        out_specs=pl.BlockSpec((bm, D_MODEL), lambda i: (i, 0)),
        out_shape=jax.ShapeDtypeStruct((n, D_MODEL), F32),
        compiler_params=_cparams("parallel"),
        name="final_stage",
    )(x2_all, p, norm_ple.reshape(1, D_MODEL), w_gate, w_ple, norm_final.reshape(1, D_MODEL))


def _moe_ffn(x1, hp, idx, gates, w1, b1, w2, b2):
    n = x1.shape[0]
    bt = ROW_TILE
    wpr = D_MODEL // 2 // LANES
    rank, counts = moe_rank(idx)
    counts = counts[0, :N_EXPERTS]
    pcounts = (counts + MOE_ROWS - 1) // MOE_ROWS * MOE_ROWS
    pends = jnp.cumsum(pcounts)
    pstarts = pends - pcounts
    e_tok = idx[:, :TOP_K]
    dest = (pstarts[e_tok] + rank[:, :TOP_K]).astype(jnp.int32).reshape(n // bt, bt * TOP_K)
    n_blocks = -(-(n * TOP_K + N_EXPERTS * (MOE_ROWS - 1)) // MOE_ROWS)
    blk_exp = jnp.minimum(jnp.searchsorted(pends, jnp.arange(n_blocks, dtype=jnp.int32) * MOE_ROWS, side='right'),
                          N_EXPERTS - 1).astype(jnp.int32)
    n_used = (pends[-1] // MOE_ROWS).astype(jnp.int32).reshape(1)
    xs = moe_dispatch(hp, dest, jnp.zeros((n_blocks * MOE_ROWS * wpr, LANES), jnp.uint32))
    y = moe_experts(xs, blk_exp, n_used, w1, b1, w2, b2)
    return moe_combine(x1, gates, dest, y)


def _row_tile(n, pref):
    return pref if n % pref == 0 else ROW_TILE


def kernel(x_prompt, x_sample, cache_k, cache_v, state_hgrn, page_table, p_prompt, p_sample, hgrn_lb, norm_mix, w_in, hgrn_norm, w_a, w_b, w_o, norm_moe, w_router, b_router, w_moe1, b_moe1, w_moe2, b_moe2, norm_ple, w_ple, w_ple_gate, norm_final):
    assert DEPTH == 1
    bsz, seq, _ = x_prompt.shape
    dbsz, tq, _ = x_sample.shape
    n_p, n_s = bsz * seq, dbsz * tq
    l = 0
    lb = jnp.cumsum(jax.nn.softmax(hgrn_lb.astype(F32), axis=0), axis=0)[l]
    w_in_b = w_in[l].astype(BF16)
    w_a_b, w_b_b, w_o_b = w_a[l].astype(BF16), w_b[l].astype(BF16), w_o[l].astype(BF16)
    w1_b, w2_b = w_moe1[l].astype(BF16), w_moe2[l].astype(BF16)
    b1 = b_moe1[l].reshape(N_EXPERTS, 1, 2 * D_FF)
    b2 = b_moe2[l].reshape(N_EXPERTS, 1, D_MODEL)
    wg_b, wp_b = w_ple_gate[l].astype(BF16), w_ple[l].astype(BF16)

    xp = x_prompt.reshape(n_p, D_MODEL)
    xs = x_sample.reshape(n_s, D_MODEL)

    def mixer_inputs(x, n):
        bm = _row_tile(n, MM_ROWS)
        h = rmsnorm_cast(x, norm_mix[l], _row_tile(n, 512))
        return matmul(h, w_in_b, bm, MM_COLS if IN_DIM % MM_COLS == 0 else B_W)

    u_p = mixer_inputs(xp, n_p)
    oa_p, st_p = hgrn2(u_p, lb, hgrn_norm[l], jnp.zeros((bsz, A_HEADS, A_KDIM, A_VDIM), F32), bsz, seq, BF16)
    rt = _row_tile(seq, 512)
    q_p, k_p, v_p = rope_qkv(u_p, _rope_tables(jnp.arange(seq)), rt if seq % rt == 0 else seq)
    ob_p = moba_prompt(q_p, k_p, v_p, bsz, seq)
    m_p = merge_branches(oa_p, ob_p, u_p, w_a_b, w_b_b, _row_tile(n_p, MM_ROWS))

    u_s = mixer_inputs(xs, n_s)
    oa_s, st_s = hgrn2(u_s, lb, hgrn_norm[l], state_hgrn[l], dbsz, tq, F32)
    tabs_s = tuple(jnp.tile(t, (dbsz, 1)) for t in _rope_tables(PAST_LEN + jnp.arange(tq)))
    q_s, k_s, v_s = rope_qkv(u_s, tabs_s, n_s)
    k_sel = min(MOBA_TOPK, PAST_LEN // MOBA_BLOCK)
    sel = moba_select(q_s, cache_k[l], page_table, tq)[:, :, :k_sel].reshape(-1)
    ob_s = moba_decode(q_s, k_s, v_s, cache_k[l], cache_v[l], page_table, sel, tq)
    m_s = merge_branches(oa_s, ob_s, u_s, w_a_b, w_b_b, _row_tile(n_s, MM_ROWS))

    x1, hp, idx, gates = out_proj_router(m_p, m_s, xp, xs, w_o_b, norm_moe[l], w_router[l], b_router[l])
    x2 = _moe_ffn(x1, hp, idx, gates, w1_b, b1, w2_b, b2)
    y_p = final_stage(x2, 0, p_prompt[l].reshape(n_p, PLE_DIM), norm_ple[l], wg_b, wp_b, norm_final,
                      _row_tile(n_p, 512))
    y_s = final_stage(x2, n_p, p_sample[l].reshape(n_s, PLE_DIM), norm_ple[l], wg_b, wp_b, norm_final, ROW_TILE)

    hs_p = (1, bsz, seq, B_HEADS, B_HDIM)
    hs_s = (1, dbsz, tq, B_HEADS, B_HDIM)
    return (y_p.reshape(bsz, seq, D_MODEL), y_s.reshape(dbsz, tq, D_MODEL),
            st_p[None], k_p.reshape(hs_p), v_p.reshape(hs_p),
            st_s[None], k_s.reshape(hs_s), v_s.reshape(hs_s))
```

```python
import functools
import math

import numpy as np
import jax
import jax.numpy as jnp
from jax import lax
from jax.experimental import pallas as pl
from jax.experimental.pallas import tpu as pltpu

D_MODEL = 2048
DEPTH = 1
PAST_LEN = 16384
PAGE_SIZE = 128
A_HEADS = 8
A_KDIM = 128
A_VDIM = 128
B_HEADS = 8
B_HDIM = 128
MOBA_BLOCK = 256
MOBA_TOPK = 3
MOBA_QBLOCK = 128
ROPE_THETA = 500000.0
ROT_DIM = B_HDIM // 4
N_EXPERTS = 32
TOP_K = 4
D_FF = D_MODEL
SWIGLU_LIMIT = 7.0
SWIGLU_ALPHA = 1.702
PLE_DIM = 256
RMS_EPS = 1e-6
NEG_BIG = -1e30

A_KW = A_HEADS * A_KDIM
A_VW = A_HEADS * A_VDIM
B_W = B_HEADS * B_HDIM
IN_WIDTHS = (A_KW, A_KW, A_VW, A_VW, B_W, B_W, B_W, D_MODEL, D_MODEL)
IN_DIM = sum(IN_WIDTHS)
IN_OFFS = tuple(int(v) for v in np.cumsum((0,) + IN_WIDTHS[:-1]))

LANES = 128
SUBLANES = 8
VMEM_LIMIT = 56 * 1024 * 1024

ROW_TILE = 256
MM_ROWS = 1024
MM_COLS = 1024
HGRN_CHUNK = 128
HGRN_HEADS = 4
MOE_ROWS = 512
MOE_FF = 512
SEL_PAGES = 8
DMA_UNROLL = 8

F32 = jnp.float32
BF16 = jnp.bfloat16
HIGHEST = lax.Precision.HIGHEST
WORDS_PER_ROW = D_MODEL // 2 // LANES
OUT_PER_ROW = D_MODEL // LANES


def _cparams(*sem):
    return pltpu.CompilerParams(dimension_semantics=sem, vmem_limit_bytes=VMEM_LIMIT)


def _dot(a, b, **kw):
    return jnp.dot(a, b, preferred_element_type=F32, **kw)


def _dot_nt(a, b, **kw):
    return lax.dot_general(a, b, (((1,), (1,)), ((), ())), preferred_element_type=F32, **kw)


def _dot_tn(a, b, **kw):
    return lax.dot_general(a, b, (((0,), (0,)), ((), ())), preferred_element_type=F32, **kw)


def _sigmoid(x):
    return 1.0 / (1.0 + jnp.exp(-x))


def _rms(x, w):
    ms = jnp.mean(x * x, axis=-1, keepdims=True)
    return x * lax.rsqrt(ms + RMS_EPS) * w


def _rmsnorm_body(x_ref, w_ref, o_ref):
    o_ref[...] = _rms(x_ref[...], w_ref[...]).astype(o_ref.dtype)


def rmsnorm_cast(x, w, bm):
    n, d = x.shape
    return pl.pallas_call(
        _rmsnorm_body,
        grid=(n // bm,),
        in_specs=[pl.BlockSpec((bm, d), lambda i: (i, 0)),
                  pl.BlockSpec((1, d), lambda i: (0, 0))],
        out_specs=pl.BlockSpec((bm, d), lambda i: (i, 0)),
        out_shape=jax.ShapeDtypeStruct((n, d), BF16),
        compiler_params=_cparams("parallel"),
        name="rmsnorm_cast",
    )(x, w.reshape(1, d))


def _matmul_body(a_ref, w_ref, o_ref):
    o_ref[...] = _dot(a_ref[...], w_ref[...]).astype(o_ref.dtype)


def _matmul_res_body(a_ref, w_ref, x_ref, o_ref):
    o_ref[...] = x_ref[...] + _dot(a_ref[...], w_ref[...])


def matmul(a, w, bm, bn, residual=None):
    n, k = a.shape
    m = w.shape[1]
    in_specs = [pl.BlockSpec((bm, k), lambda j, i: (i, 0)),
                pl.BlockSpec((k, bn), lambda j, i: (0, j))]
    args = (a, w)
    if residual is not None:
        in_specs.append(pl.BlockSpec((bm, bn), lambda j, i: (i, j)))
        args = (a, w, residual)
    return pl.pallas_call(
        _matmul_body if residual is None else _matmul_res_body,
        grid=(m // bn, n // bm),
        in_specs=in_specs,
        out_specs=pl.BlockSpec((bm, bn), lambda j, i: (i, j)),
        out_shape=jax.ShapeDtypeStruct((n, m), F32),
        compiler_params=_cparams("parallel", "parallel"),
        name="in_proj" if residual is None else "out_proj",
    )(*args)


def _rope_tables(pos):
    half = ROT_DIM // 2
    inv_freq = jnp.power(ROPE_THETA, -jnp.arange(half, dtype=F32) * (2.0 / ROT_DIM))
    ang = pos.astype(F32)[:, None] * inv_freq[None, :]
    cos, sin = jnp.cos(ang), jnp.sin(ang)
    n = pos.shape[0]
    ones = jnp.ones((n, B_HDIM - ROT_DIM), F32)
    zeros = jnp.zeros((n, B_HDIM - ROT_DIM), F32)
    zh = jnp.zeros((n, half), F32)
    c = jnp.concatenate([cos, cos, ones], axis=1)
    s_up = jnp.concatenate([-sin, zh, zeros], axis=1)
    s_dn = jnp.concatenate([zh, sin, zeros], axis=1)
    return c, s_up, s_dn


def _rope_body(q_ref, k_ref, v_ref, c_ref, su_ref, sd_ref, qo_ref, ko_ref, vo_ref):
    c, su, sd = c_ref[...], su_ref[...], sd_ref[...]
    half = ROT_DIM // 2
    for h in range(B_HEADS):
        sl = slice(h * B_HDIM, (h + 1) * B_HDIM)
        for src, dst in ((q_ref, qo_ref), (k_ref, ko_ref)):
            x = src[:, sl]
            y = x * c + pltpu.roll(x, B_HDIM - half, 1) * su + pltpu.roll(x, half, 1) * sd
            dst[:, sl] = y
    vo_ref[...] = v_ref[...]


def rope_qkv(u, tabs, bm):
    n = u.shape[0]
    c, su, sd = tabs
    tb = c.shape[0] // bm
    cq, ck, cv = (IN_OFFS[4] // B_W, IN_OFFS[5] // B_W, IN_OFFS[6] // B_W)
    tab_spec = pl.BlockSpec((bm, B_HDIM), lambda i: (i % tb, 0))
    out_spec = pl.BlockSpec((bm, B_W), lambda i: (i, 0))
    out_sds = jax.ShapeDtypeStruct((n, B_W), F32)
    return pl.pallas_call(
        _rope_body,
        grid=(n // bm,),
        in_specs=[pl.BlockSpec((bm, B_W), lambda i: (i, cq)),
                  pl.BlockSpec((bm, B_W), lambda i: (i, ck)),
                  pl.BlockSpec((bm, B_W), lambda i: (i, cv)),
                  tab_spec, tab_spec, tab_spec],
        out_specs=(out_spec, out_spec, out_spec),
        out_shape=(out_sds, out_sds, out_sds),
        compiler_params=_cparams("parallel"),
        name="rope_qkv",
    )(u, u, u, c, su, sd)


def _level_matrix(c):
    t = np.arange(c)[:, None]
    s = np.arange(c)[None, :]
    x = np.bitwise_xor(t, s)
    lvl = np.where(x > 0, np.floor(np.log2(np.maximum(x, 1))).astype(np.int32) + 1, 0)
    return np.where(s > t, -1, lvl).astype(np.int32)


def _hgrn_body(uq_ref, uf_ref, ui_ref, ug_ref, lb_ref, nw_ref, s0_ref, lvl_ref,
               o_ref, sout_ref, st_ref, *, rows, chunk, nchunks, heads):
    ci = pl.program_id(2)

    def padded(x, fill):
        if rows == chunk:
            return x
        return jnp.concatenate([x, jnp.full((chunk - rows, x.shape[1]), fill, x.dtype)], axis=0)

    rowi = lax.broadcasted_iota(jnp.int32, (chunk, chunk), 0)
    coli = lax.broadcasted_iota(jnp.int32, (chunk, chunk), 1)
    tri = jnp.where(rowi >= coli, 1.0, 0.0).astype(BF16)
    rsub = lax.broadcasted_iota(jnp.int32, (chunk, A_KDIM), 0)
    lvl = lvl_ref[...]

    @pl.when(ci == 0)
    def _():
        for hh in range(heads):
            st_ref[hh] = s0_ref[hh].T

    acts, terms = [], []
    for hh in range(heads):
        sl = slice(hh * LANES, (hh + 1) * LANES)
        lb = lb_ref[:, sl]
        uq = uq_ref[:, sl]
        q = padded(uq * _sigmoid(uq) * (A_KDIM ** -0.5), 0.0)
        f = padded(lb + (1.0 - lb) * _sigmoid(uf_ref[:, sl]), 1.0)
        g = jnp.log(f)
        acts.append((q, 1.0 - f, padded(ui_ref[:, sl], 0.0).astype(BF16)))
        hi = g.astype(BF16)
        r1 = g - hi.astype(F32)
        mid = r1.astype(BF16)
        terms += [hi, mid, (r1 - mid.astype(F32)).astype(BF16)]
    cums = _dot(tri, jnp.concatenate(terms, axis=1))

    new_states = []
    for hh in range(heads):
        sl = slice(hh * LANES, (hh + 1) * LANES)
        q, kk, vb = acts[hh]
        c0 = 3 * hh * LANES
        b = cums[:, c0:c0 + LANES] + cums[:, c0 + LANES:c0 + 2 * LANES] + cums[:, c0 + 2 * LANES:c0 + 3 * LANES]

        scores = jnp.where(lvl == 0, _dot_nt(q.astype(BF16), kk.astype(BF16)), 0.0)
        x = b
        blk, level = 1, 1
        while blk < chunk:
            nxt = pltpu.roll(x, chunk - blk, 0)
            qt = (q * jnp.exp(jnp.minimum(b - x, 0.0))).astype(BF16)
            kt = (kk * jnp.exp(jnp.minimum(nxt - b, 0.0))).astype(BF16)
            scores = jnp.where(lvl == level, _dot_nt(qt, kt), scores)
            x = jnp.where((rsub & blk) != 0, pltpu.roll(x, blk, 0), x)
            blk *= 2
            level += 1

        st = st_ref[hh]
        o = _dot(scores.astype(BF16), vb) + _dot_nt((q * jnp.exp(b)).astype(BF16), st.astype(BF16))
        b_end = b[chunk - 1:chunk, :]
        kd = (kk * jnp.exp(b_end - b)).astype(BF16)
        st_new = st * jnp.exp(b_end) + _dot_tn(vb, kd)
        st_ref[hh] = st_new
        new_states.append(st_new)

        o = o[:rows]
        ug = ug_ref[:, sl]
        o = o * lax.rsqrt(jnp.mean(o * o, axis=-1, keepdims=True) + RMS_EPS) * nw_ref[...] * (ug * _sigmoid(ug))
        o_ref[:, sl] = o.astype(o_ref.dtype)

    @pl.when(ci == nchunks - 1)
    def _():
        for hh in range(heads):
            sout_ref[hh] = new_states[hh].T


def hgrn2(u, lb, norm_w, s0, bsz, seq, out_dtype):
    assert A_KDIM == LANES and A_VDIM == LANES
    rows = min(seq, HGRN_CHUNK)
    assert seq % rows == 0
    chunk = max(rows, 2 * SUBLANES)
    nchunks = seq // rows
    lvl = jnp.asarray(_level_matrix(chunk))
    hg = min(HGRN_HEADS, A_HEADS)
    assert A_HEADS % hg == 0
    width = hg * LANES

    def u_spec(group):
        assert IN_OFFS[group] % width == 0
        off = IN_OFFS[group] // width
        return pl.BlockSpec((rows, width), lambda b, hh, c: (b * nchunks + c, off + hh))

    vec_spec = pl.BlockSpec((1, width), lambda b, hh, c: (0, hh))
    state_spec = pl.BlockSpec((None, hg, A_KDIM, A_VDIM), lambda b, hh, c: (b, hh, 0, 0))
    return pl.pallas_call(
        functools.partial(_hgrn_body, rows=rows, chunk=chunk, nchunks=nchunks, heads=hg),
        grid=(bsz, A_HEADS // hg, nchunks),
        in_specs=[u_spec(0), u_spec(1), u_spec(2), u_spec(3), vec_spec,
                  pl.BlockSpec((1, LANES), lambda b, hh, c: (0, 0)),
                  state_spec,
                  pl.BlockSpec((chunk, chunk), lambda b, hh, c: (0, 0))],
        out_specs=(pl.BlockSpec((rows, width), lambda b, hh, c: (b * nchunks + c, hh)), state_spec),
        out_shape=(jax.ShapeDtypeStruct((bsz * seq, A_VW), out_dtype),
                   jax.ShapeDtypeStruct((bsz, A_HEADS, A_KDIM, A_VDIM), F32)),
        scratch_shapes=[pltpu.VMEM((hg, A_VDIM, A_KDIM), F32)],
        compiler_params=_cparams("parallel", "parallel", "arbitrary"),
        name="hgrn2",
    )(u, u, u, u, lb.reshape(1, A_KW), norm_w.reshape(1, A_VDIM), s0, lvl)


def _moba_prompt_body(q_ref, k_ref, v_ref, cb_ref, o_ref, kb_ref, vb_ref, km_ref, *, seq, nblk):
    kf = k_ref[...]
    kb_ref[...] = kf.astype(BF16)
    vb_ref[...] = v_ref[...].astype(BF16)
    km_ref[...] = jnp.zeros_like(km_ref)
    km_ref[0:nblk, :] = jnp.sum(kf.reshape(nblk, MOBA_BLOCK, B_HDIM), axis=1) * (1.0 / MOBA_BLOCK)

    qb, kb = MOBA_QBLOCK, MOBA_BLOCK
    k_sel = min(MOBA_TOPK, nblk - 1)
    c2 = (B_HDIM ** -0.5) * math.log2(math.e)
    lane = lax.broadcasted_iota(jnp.int32, (qb, LANES), 1)

    for qi in range(seq // qb):
        q0 = qi * qb
        own = q0 // kb
        width = (own + 1) * kb
        qs = q_ref[q0:q0 + qb, :]
        s = _dot_nt(qs.astype(BF16), kb_ref[0:width, :])
        parts = []
        if own > k_sel:
            gate = _dot_nt(qs, km_ref[...], precision=HIGHEST)
            gm = jnp.where(lane < own, gate, NEG_BIG)
            rank = jnp.zeros((qb, LANES), jnp.int32)
            for n in range(own):
                cm = gm[:, n:n + 1]
                ahead = (cm > gm) | ((cm == gm) & (lane > n))
                rank = rank + jnp.where(ahead, 1, 0)
            bias = jnp.where(rank < k_sel, 0.0, NEG_BIG)
            for n in range(own):
                parts.append(s[:, n * kb:(n + 1) * kb] + bias[:, n:n + 1])
        elif own > 0:
            parts.append(s[:, :own * kb])
        parts.append(s[:, own * kb:] + cb_ref[(q0 - own * kb) // qb])
        s = jnp.concatenate(parts, axis=1)
        m = jnp.max(s, axis=-1, keepdims=True)
        p = jnp.exp2((s - m) * c2)
        l = jnp.sum(p, axis=-1, keepdims=True)
        o = _dot(p.astype(BF16), vb_ref[0:width, :]) / l
        o_ref[q0:q0 + qb, :] = o.astype(o_ref.dtype)


def moba_prompt(q, k, v, bsz, seq):
    assert seq % MOBA_BLOCK == 0 and B_HDIM == LANES and seq // MOBA_BLOCK <= LANES
    assert MOBA_BLOCK % MOBA_QBLOCK == 0
    nblk = seq // MOBA_BLOCK
    per = MOBA_BLOCK // MOBA_QBLOCK
    r = np.arange(MOBA_QBLOCK)[None, :, None] + MOBA_QBLOCK * np.arange(per)[:, None, None]
    causal = np.where(np.arange(MOBA_BLOCK)[None, None, :] <= r, 0.0, NEG_BIG).astype(np.float32)
    spec = pl.BlockSpec((seq, B_HDIM), lambda b, h: (b, h))
    return pl.pallas_call(
        functools.partial(_moba_prompt_body, seq=seq, nblk=nblk),
        grid=(bsz, B_HEADS),
        in_specs=[spec, spec, spec,
                  pl.BlockSpec((per, MOBA_QBLOCK, MOBA_BLOCK), lambda b, h: (0, 0, 0))],
        out_specs=spec,
        out_shape=jax.ShapeDtypeStruct((bsz * seq, B_W), BF16),
        scratch_shapes=[pltpu.VMEM((seq, B_HDIM), BF16), pltpu.VMEM((seq, B_HDIM), BF16),
                        pltpu.VMEM((LANES, B_HDIM), F32)],
        compiler_params=_cparams("parallel", "parallel"),
        name="moba_prompt",
    )(q, k, v, jnp.asarray(causal))


def _moba_select_body(pt_ref, *refs, tq, nblk, npages_step):
    page_refs = refs[:npages_step]
    q_ref, o_ref, km_ref = refs[npages_step:]
    j = pl.program_id(1)
    per_block = MOBA_BLOCK // PAGE_SIZE
    blocks_step = npages_step // per_block
    for i in range(blocks_step):
        acc = jnp.sum(page_refs[i * per_block][...], axis=0)
        for p in range(1, per_block):
            acc = acc + jnp.sum(page_refs[i * per_block + p][...], axis=0)
        row0 = pl.multiple_of((j * blocks_step + i) * B_HEADS, B_HEADS)
        km_ref[pl.ds(row0, B_HEADS), :] = acc * (1.0 / MOBA_BLOCK)

    @pl.when(j == pl.num_programs(1) - 1)
    def _():
        k_sel = min(MOBA_TOPK, nblk)
        lane_g = lax.broadcasted_iota(jnp.int32, (tq, nblk), 1)
        lane_o = lax.broadcasted_iota(jnp.int32, (tq, LANES), 1)
        for h in range(B_HEADS):
            kmh = km_ref[pl.ds(h, nblk, stride=B_HEADS), :]
            g = _dot_nt(q_ref[:, h * B_HDIM:(h + 1) * B_HDIM], kmh, precision=HIGHEST)
            res = jnp.zeros((tq, LANES), jnp.int32)
            for r in range(k_sel):
                mx = jnp.max(g, axis=-1, keepdims=True)
                ix = jnp.min(jnp.where(g == mx, lane_g, nblk), axis=-1, keepdims=True)
                res = jnp.where(lane_o == r, ix, res)
                g = jnp.where(lane_g == ix, -jnp.inf, g)
            o_ref[h * tq:(h + 1) * tq, :] = res


def moba_select(q, cache_k, page_table, tq):
    dbsz, npages = page_table.shape
    nblk = (PAST_LEN // MOBA_BLOCK)
    assert PAST_LEN % MOBA_BLOCK == 0 and npages * PAGE_SIZE == PAST_LEN
    ps = min(SEL_PAGES, npages)
    assert npages % ps == 0 and ps % (MOBA_BLOCK // PAGE_SIZE) == 0

    def page_spec(i):
        return pl.BlockSpec((None, PAGE_SIZE, B_HEADS, B_HDIM),
                            lambda b, j, pt: (pt[b, j * ps + i], 0, 0, 0))

    grid_spec = pltpu.PrefetchScalarGridSpec(
        num_scalar_prefetch=1,
        grid=(dbsz, npages // ps),
        in_specs=[page_spec(i) for i in range(ps)]
        + [pl.BlockSpec((tq, B_W), lambda b, j, pt: (b, 0))],
        out_specs=pl.BlockSpec((None, B_HEADS * tq, LANES), lambda b, j, pt: (b, 0, 0)),
        scratch_shapes=[pltpu.VMEM((nblk * B_HEADS, B_HDIM), F32)],
    )
    return pl.pallas_call(
        functools.partial(_moba_select_body, tq=tq, nblk=nblk, npages_step=ps),
        grid_spec=grid_spec,
        out_shape=jax.ShapeDtypeStruct((dbsz, B_HEADS * tq, LANES), jnp.int32),
        compiler_params=_cparams("parallel", "arbitrary"),
        name="moba_select",
    )(page_table, *([cache_k] * ps), q)


def _moba_decode_body(pt_ref, ix_ref, q_ref, kn_ref, vn_ref, ck_ref, cv_ref, o_ref,
                      kbuf, vbuf, sems, *, tq, k_sel, nsteps):
    b = pl.program_id(0)
    h = pl.program_id(1)
    step = b * B_HEADS + h
    slot = step % 2
    per_block = MOBA_BLOCK // PAGE_SIZE
    nsel = tq * k_sel * MOBA_BLOCK
    own_rows = kbuf.shape[1] - nsel

    def copies(st, sl):
        bb, hh = st // B_HEADS, st % B_HEADS
        out = []
        for t in range(tq):
            for s in range(k_sel):
                blk = ix_ref[(st * tq + t) * k_sel + s]
                for p in range(per_block):
                    page = pt_ref[bb, blk * per_block + p]
                    dst = pl.ds(((t * k_sel + s) * per_block + p) * PAGE_SIZE, PAGE_SIZE)
                    out.append(pltpu.make_async_copy(ck_ref.at[page, :, hh, :], kbuf.at[sl, dst, :], sems.at[0, sl]))
                    out.append(pltpu.make_async_copy(cv_ref.at[page, :, hh, :], vbuf.at[sl, dst, :], sems.at[1, sl]))
        return out

    @pl.when(step == 0)
    def _():
        for cp in copies(step, slot):
            cp.start()

    @pl.when(step + 1 < nsteps)
    def _():
        for cp in copies(step + 1, 1 - slot):
            cp.start()

    zpad = jnp.zeros((own_rows - tq, B_HDIM), F32)
    kbuf[slot, pl.ds(nsel, own_rows), :] = jnp.concatenate([kn_ref[...], zpad], axis=0)
    vbuf[slot, pl.ds(nsel, own_rows), :] = jnp.concatenate([vn_ref[...], zpad], axis=0)
    for cp in copies(step, slot):
        cp.wait()

    ncol = kbuf.shape[1]
    qb = q_ref[...].astype(BF16)
    s = _dot_nt(qb, kbuf[slot].astype(BF16)) * (B_HDIM ** -0.5)
    col = lax.broadcasted_iota(jnp.int32, (tq, ncol), 1)
    row = lax.broadcasted_iota(jnp.int32, (tq, ncol), 0)
    lo = row * (k_sel * MOBA_BLOCK)
    allowed = ((col >= lo) & (col < lo + k_sel * MOBA_BLOCK)) | ((col >= nsel) & (col <= nsel + row))
    s = jnp.where(allowed, s, NEG_BIG)
    m = jnp.max(s, axis=-1, keepdims=True)
    p = jnp.exp(s - m)
    l = jnp.sum(p, axis=-1, keepdims=True)
    o_ref[...] = _dot(p.astype(BF16), vbuf[slot].astype(BF16)) / l


def moba_decode(q, kn, vn, cache_k, cache_v, page_table, sel, tq):
    dbsz = page_table.shape[0]
    k_sel = min(MOBA_TOPK, PAST_LEN // MOBA_BLOCK)
    assert PAST_LEN % MOBA_BLOCK == 0, "cached rows of the own block are not supported"
    assert k_sel > 0 and tq <= LANES
    nrows = tq * k_sel * MOBA_BLOCK + LANES
    spec = pl.BlockSpec((tq, B_HDIM), lambda b, h, pt, ix: (b, h))
    grid_spec = pltpu.PrefetchScalarGridSpec(
        num_scalar_prefetch=2,
        grid=(dbsz, B_HEADS),
        in_specs=[spec, spec, spec,
                  pl.BlockSpec(memory_space=pl.ANY), pl.BlockSpec(memory_space=pl.ANY)],
        out_specs=spec,
        scratch_shapes=[pltpu.VMEM((2, nrows, B_HDIM), F32), pltpu.VMEM((2, nrows, B_HDIM), F32),
                        pltpu.SemaphoreType.DMA((2, 2))],
    )
    return pl.pallas_call(
        functools.partial(_moba_decode_body, tq=tq, k_sel=k_sel, nsteps=dbsz * B_HEADS),
        grid_spec=grid_spec,
        out_shape=jax.ShapeDtypeStruct((dbsz * tq, B_W), F32),
        compiler_params=_cparams("arbitrary", "arbitrary"),
        name="moba_decode",
    )(page_table, sel, q, kn, vn, cache_k, cache_v)


def _merge_body(oa_ref, ob_ref, ga_ref, gb_ref, wa_ref, wb_ref, o_ref):
    a = _dot(oa_ref[...].astype(BF16), wa_ref[...])
    b = _dot(ob_ref[...].astype(BF16), wb_ref[...])
    o_ref[...] = (_sigmoid(ga_ref[...]) * a + _sigmoid(gb_ref[...]) * b).astype(o_ref.dtype)


def merge_branches(o_a, o_b, u, w_a, w_b, bm):
    n = o_a.shape[0]
    bn = min(MM_COLS, D_MODEL)
    assert IN_OFFS[7] % bn == 0 and IN_OFFS[8] % bn == 0 and D_MODEL % bn == 0
    ca, cb = IN_OFFS[7] // bn, IN_OFFS[8] // bn
    return pl.pallas_call(
        _merge_body,
        grid=(D_MODEL // bn, n // bm),
        in_specs=[pl.BlockSpec((bm, A_VW), lambda j, i: (i, 0)),
                  pl.BlockSpec((bm, B_W), lambda j, i: (i, 0)),
                  pl.BlockSpec((bm, bn), lambda j, i: (i, ca + j)),
                  pl.BlockSpec((bm, bn), lambda j, i: (i, cb + j)),
                  pl.BlockSpec((A_VW, bn), lambda j, i: (0, j)),
                  pl.BlockSpec((B_W, bn), lambda j, i: (0, j))],
        out_specs=pl.BlockSpec((bm, bn), lambda j, i: (i, j)),
        out_shape=jax.ShapeDtypeStruct((n, D_MODEL), BF16),
        compiler_params=_cparams("parallel", "parallel"),
        name="merge_branches",
    )(o_a, o_b, u, u, w_a, w_b)


def _pack_words(h):
    half = h.shape[1] // 2
    lo = pltpu.bitcast(h[:, :half].astype(BF16).astype(F32), jnp.uint32)
    hi = pltpu.bitcast(h[:, half:].astype(BF16).astype(F32), jnp.uint32)
    return (lo >> 16) | (hi & jnp.uint32(0xFFFF0000))


def _router_body(xp_ref, xs_ref, nw_ref, wr_ref, br_ref, hp_ref, idx_ref, gate_ref, *, n_prompt_tiles):
    i = pl.program_id(0)
    x1 = jnp.where(i < n_prompt_tiles, xp_ref[...], xs_ref[...])
    h2 = _rms(x1, nw_ref[...])
    words = _pack_words(h2)
    for s in range(WORDS_PER_ROW):
        hp_ref[s] = words[:, s * LANES:(s + 1) * LANES]
    h_hi = h2.astype(BF16)
    h_lo = (h2 - h_hi.astype(F32)).astype(BF16)
    a = _dot(h_hi, wr_ref[...])
    b = _dot(h_lo, wr_ref[...])
    logits = a + pltpu.roll(a, LANES - N_EXPERTS, 1) + b + br_ref[...]
    lane = lax.broadcasted_iota(jnp.int32, logits.shape, 1)
    g = jnp.where(lane < N_EXPERTS, logits, -jnp.inf)
    idx = jnp.zeros(logits.shape, jnp.int32)
    val = jnp.zeros(logits.shape, F32)
    top = None
    for r in range(TOP_K):
        mx = jnp.max(g, axis=-1, keepdims=True)
        ix = jnp.min(jnp.where(g == mx, lane, LANES), axis=-1, keepdims=True)
        top = mx if top is None else top
        idx = jnp.where(lane == r, ix, idx)
        val = jnp.where(lane == r, jnp.exp(mx - top), val)
        g = jnp.where(lane == ix, -jnp.inf, g)
    idx_ref[...] = idx
    gate_ref[...] = val / jnp.sum(val, axis=-1, keepdims=True)


def _dual_maps(tp):
    return (lambda i: (jnp.minimum(i, tp - 1), 0)), (lambda i: (jnp.maximum(i - tp, 0), 0))


def moe_router(x1_p, x1_s, norm_w, w_router, b_router):
    bm = ROW_TILE
    n_p, n_s = x1_p.shape[0], x1_s.shape[0]
    assert n_p % bm == 0 and n_s % bm == 0 and 2 * N_EXPERTS <= LANES and TOP_K <= LANES
    tp, ts = n_p // bm, n_s // bm
    n = n_p + n_s
    w_hi = w_router.astype(BF16)
    w_lo = (w_router - w_hi.astype(F32)).astype(BF16)
    wr = jnp.zeros((D_MODEL, LANES), BF16).at[:, :N_EXPERTS].set(w_hi).at[:, N_EXPERTS:2 * N_EXPERTS].set(w_lo)
    br = jnp.zeros((1, LANES), F32).at[0, :N_EXPERTS].set(b_router)
    p_map, s_map = _dual_maps(tp)
    full = lambda i: (0, 0)
    row = lambda i: (i, 0)
    return pl.pallas_call(
        functools.partial(_router_body, n_prompt_tiles=tp),
        grid=(tp + ts,),
        in_specs=[pl.BlockSpec((bm, D_MODEL), p_map), pl.BlockSpec((bm, D_MODEL), s_map),
                  pl.BlockSpec((1, D_MODEL), full),
                  pl.BlockSpec((D_MODEL, LANES), full), pl.BlockSpec((1, LANES), full)],
        out_specs=(pl.BlockSpec((WORDS_PER_ROW, bm, LANES), lambda i: (0, i, 0)),
                   pl.BlockSpec((bm, LANES), row), pl.BlockSpec((bm, LANES), row)),
        out_shape=(jax.ShapeDtypeStruct((WORDS_PER_ROW, n, LANES), jnp.uint32),
                   jax.ShapeDtypeStruct((n, LANES), jnp.int32),
                   jax.ShapeDtypeStruct((n, LANES), F32)),
        compiler_params=_cparams("parallel"),
        name="moe_router",
    )(x1_p, x1_s, norm_w.reshape(1, D_MODEL), wr, br)


def _rank_body(idx_ref, rank_ref, cnt_ref, carry_ref):
    i = pl.program_id(0)

    @pl.when(i == 0)
    def _():
        carry_ref[...] = jnp.zeros_like(carry_ref)

    idx = idx_ref[...]
    bt = idx.shape[0]
    lane = lax.broadcasted_iota(jnp.int32, (bt, LANES), 1)
    rowi = lax.broadcasted_iota(jnp.int32, (bt, bt), 0)
    coli = lax.broadcasted_iota(jnp.int32, (bt, bt), 1)
    before = jnp.where(rowi > coli, 1.0, 0.0).astype(BF16)
    base = carry_ref[...]
    res = jnp.zeros((bt, LANES), jnp.int32)
    for k in range(TOP_K):
        onehot = jnp.where(lane == idx[:, k:k + 1], 1.0, 0.0)
        earlier = _dot(before, onehot.astype(BF16))
        rk = jnp.sum(onehot * (earlier + base), axis=-1, keepdims=True)
        res = jnp.where(lane == k, rk.astype(jnp.int32), res)
        base = base + jnp.sum(onehot, axis=0, keepdims=True)
    rank_ref[...] = res
    carry_ref[...] = base
    cnt_ref[...] = jnp.broadcast_to(base, cnt_ref.shape).astype(jnp.int32)


def moe_rank(idx):
    n = idx.shape[0]
    bt = ROW_TILE
    return pl.pallas_call(
        _rank_body,
        grid=(n // bt,),
        in_specs=[pl.BlockSpec((bt, LANES), lambda i: (i, 0))],
        out_specs=(pl.BlockSpec((bt, LANES), lambda i: (i, 0)),
                   pl.BlockSpec((SUBLANES, LANES), lambda i: (0, 0))),
        out_shape=(jax.ShapeDtypeStruct((n, LANES), jnp.int32),
                   jax.ShapeDtypeStruct((SUBLANES, LANES), jnp.int32)),
        scratch_shapes=[pltpu.VMEM((1, LANES), F32)],
        compiler_params=_cparams("arbitrary"),
        name="moe_rank",
    )(idx)


def _load_slots(dest_ref, dsm, sem_idx):
    cp = pltpu.make_async_copy(dest_ref.at[pl.program_id(0)], dsm, sem_idx)
    cp.start()
    cp.wait()


def _dispatch_body(hp_ref, dest_ref, xs_in_ref, xs_ref, dsm, sem_idx, sem, *, bt):
    del xs_in_ref
    wpr = WORDS_PER_ROW
    _load_slots(dest_ref, dsm, sem_idx)

    def issue(r, c):
        for k in range(TOP_K):
            d = dsm[r * TOP_K + k]
            pltpu.make_async_copy(hp_ref.at[:, r, :],
                                  xs_ref.at[pl.ds(pl.multiple_of(d * wpr, wpr), wpr), :], sem).start()
        return c

    lax.fori_loop(0, bt, issue, 0, unroll=DMA_UNROLL)
    for k in range(TOP_K):
        whole = xs_ref.at[pl.ds(0, bt * wpr), :]
        pltpu.make_async_copy(whole, whole, sem).wait()


def moe_dispatch(hp, dest, xs_init):
    bt = ROW_TILE
    nt = dest.shape[0]
    return pl.pallas_call(
        functools.partial(_dispatch_body, bt=bt),
        grid=(nt,),
        in_specs=[pl.BlockSpec((WORDS_PER_ROW, bt, LANES), lambda i: (0, i, 0)),
                  pl.BlockSpec(memory_space=pl.ANY),
                  pl.BlockSpec(memory_space=pl.ANY)],
        out_specs=pl.BlockSpec(memory_space=pl.ANY),
        out_shape=jax.ShapeDtypeStruct(xs_init.shape, xs_init.dtype),
        scratch_shapes=[pltpu.SMEM((bt * TOP_K,), jnp.int32), pltpu.SemaphoreType.DMA,
                        pltpu.SemaphoreType.DMA],
        input_output_aliases={2: 0},
        compiler_params=_cparams("arbitrary"),
        name="moe_dispatch",
    )(hp, dest, xs_init)


def _experts_body(be_ref, nu_ref, x_ref, w1g_ref, w1l_ref, b1g_ref, b1l_ref, w2_ref, b2_ref,
                  o_ref, xb_ref, acc_ref, *, rows):
    i = pl.program_id(0)
    c = pl.program_id(1)
    wpr, opr = WORDS_PER_ROW, OUT_PER_ROW

    @pl.when(i < nu_ref[0])
    def _():
        @pl.when(c == 0)
        def _():
            half = wpr * LANES
            for s in range(wpr):
                w = x_ref[pl.ds(s, rows, stride=wpr), :]
                lo = pltpu.bitcast(w << 16, F32)
                hi = pltpu.bitcast(w & jnp.uint32(0xFFFF0000), F32)
                xb_ref[:, s * LANES:(s + 1) * LANES] = lo.astype(BF16)
                xb_ref[:, half + s * LANES:half + (s + 1) * LANES] = hi.astype(BF16)
            acc_ref[...] = jnp.broadcast_to(b2_ref[...], acc_ref.shape)

        xb = xb_ref[...]
        gu = _dot(xb, w1g_ref[...]) + b1g_ref[...]
        li = _dot(xb, w1l_ref[...]) + b1l_ref[...]
        glu = jnp.minimum(gu, SWIGLU_LIMIT)
        lin = jnp.clip(li, -SWIGLU_LIMIT, SWIGLU_LIMIT)
        act = glu * _sigmoid(SWIGLU_ALPHA * glu) * (lin + 1.0)
        acc_ref[...] += _dot(act.astype(BF16), w2_ref[...])

        @pl.when(c == pl.num_programs(1) - 1)
        def _():
            acc = acc_ref[...]
            for s in range(opr):
                o_ref[pl.ds(s, rows, stride=opr), :] = acc[:, s * LANES:(s + 1) * LANES]

    @pl.when((i >= nu_ref[0]) & (c == 0))
    def _():
        o_ref[...] = jnp.zeros_like(o_ref)


def moe_experts(xs, blk_exp, n_used, w1, b1, w2, b2):
    rows, ff = MOE_ROWS, min(MOE_FF, D_FF)
    wpr, opr = WORDS_PER_ROW, OUT_PER_ROW
    n_rows = xs.shape[0] // wpr
    nb = n_rows // rows
    nc = D_FF // ff
    lin_off = D_FF // ff

    def eff(i, c, be, nu):
        live = i < nu[0]
        return jnp.where(live, i, nu[0] - 1), jnp.where(live, c, nc - 1)

    def x_map(i, c, be, nu):
        return (eff(i, c, be, nu)[0], 0)

    def w1g_map(i, c, be, nu):
        ii, cc = eff(i, c, be, nu)
        return (be[ii], 0, cc)

    def w1l_map(i, c, be, nu):
        ii, cc = eff(i, c, be, nu)
        return (be[ii], 0, lin_off + cc)

    def w2_map(i, c, be, nu):
        ii, cc = eff(i, c, be, nu)
        return (be[ii], cc, 0)

    def b2_map(i, c, be, nu):
        return (be[eff(i, c, be, nu)[0]], 0, 0)

    grid_spec = pltpu.PrefetchScalarGridSpec(
        num_scalar_prefetch=2,
        grid=(nb, nc),
        in_specs=[pl.BlockSpec((rows * wpr, LANES), x_map),
                  pl.BlockSpec((None, D_MODEL, ff), w1g_map),
                  pl.BlockSpec((None, D_MODEL, ff), w1l_map),
                  pl.BlockSpec((None, 1, ff), w1g_map),
                  pl.BlockSpec((None, 1, ff), w1l_map),
                  pl.BlockSpec((None, ff, D_MODEL), w2_map),
                  pl.BlockSpec((None, 1, D_MODEL), b2_map)],
        out_specs=pl.BlockSpec((rows * opr, LANES), lambda i, c, be, nu: (i, 0)),
        scratch_shapes=[pltpu.VMEM((rows, D_MODEL), BF16), pltpu.VMEM((rows, D_MODEL), F32)],
    )
    return pl.pallas_call(
        functools.partial(_experts_body, rows=rows),
        grid_spec=grid_spec,
        out_shape=jax.ShapeDtypeStruct((n_rows * opr, LANES), F32),
        compiler_params=_cparams("arbitrary", "arbitrary"),
        name="moe_experts",
    )(blk_exp, n_used, xs, w1, w1, b1, b1, w2, b2)


def _combine_body(xp_ref, xs_ref, gate_ref, dest_ref, y_ref, o_ref, buf, dsm, sem_idx, sem,
                  *, bt, n_prompt_tiles):
    opr = OUT_PER_ROW
    _load_slots(dest_ref, dsm, sem_idx)

    def issue(r, c):
        for k in range(TOP_K):
            d = dsm[r * TOP_K + k]
            pltpu.make_async_copy(y_ref.at[pl.ds(pl.multiple_of(d * opr, opr), opr), :],
                                  buf.at[k, :, r, :], sem).start()
        return c

    lax.fori_loop(0, bt, issue, 0, unroll=DMA_UNROLL)
    for k in range(TOP_K):
        whole = y_ref.at[pl.ds(0, bt * opr), :]
        pltpu.make_async_copy(whole, whole, sem).wait()

    gates = gate_ref[...]
    first = pl.program_id(0) < n_prompt_tiles
    for s in range(opr):
        sl = slice(s * LANES, (s + 1) * LANES)
        acc = jnp.where(first, xp_ref[:, sl], xs_ref[:, sl])
        for k in range(TOP_K):
            acc = acc + gates[:, k:k + 1] * buf[k, s]
        o_ref[:, sl] = acc


def moe_combine(x1_p, x1_s, gates, dest, y):
    bt = ROW_TILE
    n = gates.shape[0]
    tp = x1_p.shape[0] // bt
    p_map, s_map = _dual_maps(tp)
    return pl.pallas_call(
        functools.partial(_combine_body, bt=bt, n_prompt_tiles=tp),
        grid=(n // bt,),
        in_specs=[pl.BlockSpec((bt, D_MODEL), p_map), pl.BlockSpec((bt, D_MODEL), s_map),
                  pl.BlockSpec((bt, LANES), lambda i: (i, 0)),
                  pl.BlockSpec(memory_space=pl.ANY),
                  pl.BlockSpec(memory_space=pl.ANY)],
        out_specs=pl.BlockSpec((bt, D_MODEL), lambda i: (i, 0)),
        out_shape=jax.ShapeDtypeStruct((n, D_MODEL), F32),
        scratch_shapes=[pltpu.VMEM((TOP_K, OUT_PER_ROW, bt, LANES), F32),
                        pltpu.SMEM((bt * TOP_K,), jnp.int32),
                        pltpu.SemaphoreType.DMA, pltpu.SemaphoreType.DMA],
        compiler_params=_cparams("arbitrary"),
        name="moe_combine",
    )(x1_p, x1_s, gates, dest, y)


def _final_body(x_ref, p_ref, nw_ref, wg_ref, wp_ref, nf_ref, o_ref):
    x = x_ref[...]
    gate = _sigmoid(_dot(_rms(x, nw_ref[...]).astype(BF16), wg_ref[...]))
    x = x + gate * _dot(p_ref[...].astype(BF16), wp_ref[...])
    o_ref[...] = _rms(x, nf_ref[...])


def final_stage(x2_all, row_off, p, norm_ple, w_gate, w_ple, norm_final, bm):
    n = p.shape[0]
    off = row_off // bm
    assert row_off % bm == 0 and n % bm == 0
    full = lambda i: (0, 0)
    return pl.pallas_call(
        _final_body,
        grid=(n // bm,),
        in_specs=[pl.BlockSpec((bm, D_MODEL), lambda i: (off + i, 0)),
                  pl.BlockSpec((bm, PLE_DIM), lambda i: (i, 0)),
                  pl.BlockSpec((1, D_MODEL), full), pl.BlockSpec((D_MODEL, D_MODEL), full),
                  pl.BlockSpec((PLE_DIM, D_MODEL), full), pl.BlockSpec((1, D_MODEL), full)],
        out_specs=pl.BlockSpec((bm, D_MODEL), lambda i: (i, 0)),
        out_shape=jax.ShapeDtypeStruct((n, D_MODEL), F32),
        compiler_params=_cparams("parallel"),
        name="final_stage",
    )(x2_all, p, norm_ple.reshape(1, D_MODEL), w_gate, w_ple, norm_final.reshape(1, D_MODEL))


def _moe_ffn(x1_p, x1_s, norm_w, w_router, b_router, w1, b1, w2, b2):
    hp, idx, gates = moe_router(x1_p, x1_s, norm_w, w_router, b_router)
    n = idx.shape[0]
    bt = ROW_TILE
    rank, counts = moe_rank(idx)
    counts = counts[0, :N_EXPERTS]
    pcounts = (counts + MOE_ROWS - 1) // MOE_ROWS * MOE_ROWS
    pends = jnp.cumsum(pcounts)
    pstarts = pends - pcounts
    e_tok = idx[:, :TOP_K]
    dest = (pstarts[e_tok] + rank[:, :TOP_K]).astype(jnp.int32).reshape(n // bt, bt * TOP_K)
    n_blocks = -(-(n * TOP_K + N_EXPERTS * (MOE_ROWS - 1)) // MOE_ROWS)
    blk_start = jnp.arange(n_blocks, dtype=jnp.int32) * MOE_ROWS
    blk_exp = jnp.minimum(jnp.sum((pends[None, :] <= blk_start[:, None]).astype(jnp.int32), axis=1),
                          N_EXPERTS - 1).astype(jnp.int32)
    n_used = (pends[-1] // MOE_ROWS).astype(jnp.int32).reshape(1)
    xs = moe_dispatch(hp, dest, jnp.zeros((n_blocks * MOE_ROWS * WORDS_PER_ROW, LANES), jnp.uint32))
    y = moe_experts(xs, blk_exp, n_used, w1, b1, w2, b2)
    return moe_combine(x1_p, x1_s, gates, dest, y)


def _row_tile(n, pref):
    return pref if n % pref == 0 else ROW_TILE


def kernel(x_prompt, x_sample, cache_k, cache_v, state_hgrn, page_table, p_prompt, p_sample, hgrn_lb, norm_mix, w_in, hgrn_norm, w_a, w_b, w_o, norm_moe, w_router, b_router, w_moe1, b_moe1, w_moe2, b_moe2, norm_ple, w_ple, w_ple_gate, norm_final):
    assert DEPTH == 1
    bsz, seq, _ = x_prompt.shape
    dbsz, tq, _ = x_sample.shape
    n_p, n_s = bsz * seq, dbsz * tq
    l = 0
    lb = jnp.cumsum(jax.nn.softmax(hgrn_lb.astype(F32), axis=0), axis=0)[l]
    w_in_b = w_in[l].astype(BF16)
    w_a_b, w_b_b, w_o_b = w_a[l].astype(BF16), w_b[l].astype(BF16), w_o[l].astype(BF16)
    w1_b, w2_b = w_moe1[l].astype(BF16), w_moe2[l].astype(BF16)
    b1 = b_moe1[l].reshape(N_EXPERTS, 1, 2 * D_FF)
    b2 = b_moe2[l].reshape(N_EXPERTS, 1, D_MODEL)
    wg_b, wp_b = w_ple_gate[l].astype(BF16), w_ple[l].astype(BF16)
    bn = MM_COLS if IN_DIM % MM_COLS == 0 and D_MODEL % MM_COLS == 0 else B_W

    xp = x_prompt.reshape(n_p, D_MODEL)
    xs = x_sample.reshape(n_s, D_MODEL)

    def mixer_inputs(x, n):
        h = rmsnorm_cast(x, norm_mix[l], _row_tile(n, 512))
        return matmul(h, w_in_b, _row_tile(n, MM_ROWS), bn)

    u_p = mixer_inputs(xp, n_p)
    oa_p, st_p = hgrn2(u_p, lb, hgrn_norm[l], jnp.zeros((bsz, A_HEADS, A_KDIM, A_VDIM), F32), bsz, seq, BF16)
    rt = _row_tile(seq, 512)
    q_p, k_p, v_p = rope_qkv(u_p, _rope_tables(jnp.arange(seq)), rt if seq % rt == 0 else seq)
    ob_p = moba_prompt(q_p, k_p, v_p, bsz, seq)
    m_p = merge_branches(oa_p, ob_p, u_p, w_a_b, w_b_b, _row_tile(n_p, MM_ROWS))
    x1_p = matmul(m_p, w_o_b, _row_tile(n_p, MM_ROWS), bn, residual=xp)

    u_s = mixer_inputs(xs, n_s)
    oa_s, st_s = hgrn2(u_s, lb, hgrn_norm[l], state_hgrn[l], dbsz, tq, F32)
    tabs_s = tuple(jnp.tile(t, (dbsz, 1)) for t in _rope_tables(PAST_LEN + jnp.arange(tq)))
    q_s, k_s, v_s = rope_qkv(u_s, tabs_s, n_s)
    k_sel = min(MOBA_TOPK, PAST_LEN // MOBA_BLOCK)
    sel = moba_select(q_s, cache_k[l], page_table, tq)[:, :, :k_sel].reshape(-1)
    ob_s = moba_decode(q_s, k_s, v_s, cache_k[l], cache_v[l], page_table, sel, tq)
    m_s = merge_branches(oa_s, ob_s, u_s, w_a_b, w_b_b, _row_tile(n_s, MM_ROWS))
    x1_s = matmul(m_s, w_o_b, _row_tile(n_s, MM_ROWS), bn, residual=xs)

    x2 = _moe_ffn(x1_p, x1_s, norm_moe[l], w_router[l], b_router[l], w1_b, b1, w2_b, b2)
    y_p = final_stage(x2, 0, p_prompt[l].reshape(n_p, PLE_DIM), norm_ple[l], wg_b, wp_b, norm_final,
                      _row_tile(n_p, 512))
    y_s = final_stage(x2, n_p, p_sample[l].reshape(n_s, PLE_DIM), norm_ple[l], wg_b, wp_b, norm_final, ROW_TILE)

    hs_p = (1, bsz, seq, B_HEADS, B_HDIM)
    hs_s = (1, dbsz, tq, B_HEADS, B_HDIM)
    return (y_p.reshape(bsz, seq, D_MODEL), y_s.reshape(dbsz, tq, D_MODEL),
            st_p[None], k_p.reshape(hs_p), v_p.reshape(hs_p),
            st_s[None], k_s.reshape(hs_s), v_s.reshape(hs_s))
```

```python
import functools
import math

import numpy as np
import jax
import jax.numpy as jnp
from jax import lax
from jax.experimental import pallas as pl
from jax.experimental.pallas import tpu as pltpu

D_MODEL = 2048
DEPTH = 1
PAST_LEN = 16384
PAGE_SIZE = 128
A_HEADS = 8
A_KDIM = 128
A_VDIM = 128
B_HEADS = 8
B_HDIM = 128
MOBA_BLOCK = 256
MOBA_TOPK = 3
MOBA_QBLOCK = 128
ROPE_THETA = 500000.0
ROT_DIM = B_HDIM // 4
N_EXPERTS = 32
TOP_K = 4
D_FF = D_MODEL
SWIGLU_LIMIT = 7.0
SWIGLU_ALPHA = 1.702
PLE_DIM = 256
RMS_EPS = 1e-6
NEG_BIG = -1e30

A_KW = A_HEADS * A_KDIM
A_VW = A_HEADS * A_VDIM
B_W = B_HEADS * B_HDIM
IN_WIDTHS = (A_KW, A_KW, A_VW, A_VW, B_W, B_W, B_W, D_MODEL, D_MODEL)
IN_DIM = sum(IN_WIDTHS)
IN_OFFS = tuple(int(v) for v in np.cumsum((0,) + IN_WIDTHS[:-1]))

LANES = 128
SUBLANES = 8
VMEM_LIMIT = 56 * 1024 * 1024

ROW_TILE = 256
MM_ROWS = 1024
MM_COLS = 1024
HGRN_CHUNK = 128
HGRN_HEADS = 8
MOE_ROWS = 512
MOE_FF = 1024
SEL_PAGES = 16
DMA_UNROLL = 8

F32 = jnp.float32
BF16 = jnp.bfloat16
HIGHEST = lax.Precision.HIGHEST
WORDS_PER_ROW = D_MODEL // 2 // LANES


def _cparams(*sem):
    return pltpu.CompilerParams(dimension_semantics=sem, vmem_limit_bytes=VMEM_LIMIT)


def _dot(a, b, **kw):
    return jnp.dot(a, b, preferred_element_type=F32, **kw)


def _dot_nt(a, b, **kw):
    return lax.dot_general(a, b, (((1,), (1,)), ((), ())), preferred_element_type=F32, **kw)


def _dot_tn(a, b, **kw):
    return lax.dot_general(a, b, (((0,), (0,)), ((), ())), preferred_element_type=F32, **kw)


def _sigmoid(x):
    return 1.0 / (1.0 + jnp.exp(-x))


def _rms(x, w):
    ms = jnp.mean(x * x, axis=-1, keepdims=True)
    return x * lax.rsqrt(ms + RMS_EPS) * w


def _rmsnorm_body(x_ref, w_ref, o_ref):
    o_ref[...] = _rms(x_ref[...], w_ref[...]).astype(o_ref.dtype)


def rmsnorm_cast(x, w, bm):
    n, d = x.shape
    return pl.pallas_call(
        _rmsnorm_body,
        grid=(n // bm,),
        in_specs=[pl.BlockSpec((bm, d), lambda i: (i, 0)),
                  pl.BlockSpec((1, d), lambda i: (0, 0))],
        out_specs=pl.BlockSpec((bm, d), lambda i: (i, 0)),
        out_shape=jax.ShapeDtypeStruct((n, d), BF16),
        compiler_params=_cparams("parallel"),
        name="rmsnorm_cast",
    )(x, w.reshape(1, d))


def _matmul_body(a_ref, w_ref, o_ref):
    o_ref[...] = _dot(a_ref[...], w_ref[...]).astype(o_ref.dtype)


def _matmul_res_body(a_ref, w_ref, x_ref, o_ref):
    o_ref[...] = x_ref[...] + _dot(a_ref[...], w_ref[...])


def matmul(a, w, bm, bn, residual=None):
    n, k = a.shape
    m = w.shape[1]
    in_specs = [pl.BlockSpec((bm, k), lambda j, i: (i, 0)),
                pl.BlockSpec((k, bn), lambda j, i: (0, j))]
    args = (a, w)
    if residual is not None:
        in_specs.append(pl.BlockSpec((bm, bn), lambda j, i: (i, j)))
        args = (a, w, residual)
    return pl.pallas_call(
        _matmul_body if residual is None else _matmul_res_body,
        grid=(m // bn, n // bm),
        in_specs=in_specs,
        out_specs=pl.BlockSpec((bm, bn), lambda j, i: (i, j)),
        out_shape=jax.ShapeDtypeStruct((n, m), F32),
        compiler_params=_cparams("parallel", "parallel"),
        name="in_proj" if residual is None else "out_proj",
    )(*args)


def _rope_tables(pos):
    half = ROT_DIM // 2
    inv_freq = jnp.power(ROPE_THETA, -jnp.arange(half, dtype=F32) * (2.0 / ROT_DIM))
    ang = pos.astype(F32)[:, None] * inv_freq[None, :]
    cos, sin = jnp.cos(ang), jnp.sin(ang)
    n = pos.shape[0]
    ones = jnp.ones((n, B_HDIM - ROT_DIM), F32)
    zeros = jnp.zeros((n, B_HDIM - ROT_DIM), F32)
    zh = jnp.zeros((n, half), F32)
    c = jnp.concatenate([cos, cos, ones], axis=1)
    s_up = jnp.concatenate([-sin, zh, zeros], axis=1)
    s_dn = jnp.concatenate([zh, sin, zeros], axis=1)
    return c, s_up, s_dn


def _rope_body(q_ref, k_ref, v_ref, c_ref, su_ref, sd_ref, qo_ref, ko_ref, vo_ref):
    c, su, sd = c_ref[...], su_ref[...], sd_ref[...]
    half = ROT_DIM // 2
    for h in range(B_HEADS):
        sl = slice(h * B_HDIM, (h + 1) * B_HDIM)
        for src, dst in ((q_ref, qo_ref), (k_ref, ko_ref)):
            x = src[:, sl]
            y = x * c + pltpu.roll(x, B_HDIM - half, 1) * su + pltpu.roll(x, half, 1) * sd
            dst[:, sl] = y
    vo_ref[...] = v_ref[...]


def rope_qkv(u, tabs, bm):
    n = u.shape[0]
    c, su, sd = tabs
    tb = c.shape[0] // bm
    cq, ck, cv = (IN_OFFS[4] // B_W, IN_OFFS[5] // B_W, IN_OFFS[6] // B_W)
    tab_spec = pl.BlockSpec((bm, B_HDIM), lambda i: (i % tb, 0))
    out_spec = pl.BlockSpec((bm, B_W), lambda i: (i, 0))
    out_sds = jax.ShapeDtypeStruct((n, B_W), F32)
    return pl.pallas_call(
        _rope_body,
        grid=(n // bm,),
        in_specs=[pl.BlockSpec((bm, B_W), lambda i: (i, cq)),
                  pl.BlockSpec((bm, B_W), lambda i: (i, ck)),
                  pl.BlockSpec((bm, B_W), lambda i: (i, cv)),
                  tab_spec, tab_spec, tab_spec],
        out_specs=(out_spec, out_spec, out_spec),
        out_shape=(out_sds, out_sds, out_sds),
        compiler_params=_cparams("parallel"),
        name="rope_qkv",
    )(u, u, u, c, su, sd)


def _level_matrix(c):
    t = np.arange(c)[:, None]
    s = np.arange(c)[None, :]
    x = np.bitwise_xor(t, s)
    lvl = np.where(x > 0, np.floor(np.log2(np.maximum(x, 1))).astype(np.int32) + 1, 0)
    return np.where(s > t, -1, lvl).astype(np.int32)


def _hgrn_body(uq_ref, uf_ref, ui_ref, ug_ref, lb_ref, nw_ref, s0_ref, lvl_ref,
               o_ref, sout_ref, st_ref, *, rows, chunk, nchunks, heads):
    ci = pl.program_id(2)

    def padded(x, fill):
        if rows == chunk:
            return x
        return jnp.concatenate([x, jnp.full((chunk - rows, x.shape[1]), fill, x.dtype)], axis=0)

    rowi = lax.broadcasted_iota(jnp.int32, (chunk, chunk), 0)
    coli = lax.broadcasted_iota(jnp.int32, (chunk, chunk), 1)
    tri = jnp.where(rowi >= coli, 1.0, 0.0).astype(BF16)
    rsub = lax.broadcasted_iota(jnp.int32, (chunk, A_KDIM), 0)
    lvl = lvl_ref[...]

    @pl.when(ci == 0)
    def _():
        for hh in range(heads):
            st_ref[hh] = s0_ref[hh].T

    acts, terms = [], []
    for hh in range(heads):
        sl = slice(hh * LANES, (hh + 1) * LANES)
        lb = lb_ref[:, sl]
        uq = uq_ref[:, sl]
        q = padded(uq * _sigmoid(uq) * (A_KDIM ** -0.5), 0.0)
        f = padded(lb + (1.0 - lb) * _sigmoid(uf_ref[:, sl]), 1.0)
        g = jnp.log(f)
        acts.append((q, 1.0 - f, padded(ui_ref[:, sl], 0.0).astype(BF16)))
        hi = g.astype(BF16)
        r1 = g - hi.astype(F32)
        mid = r1.astype(BF16)
        terms += [hi, mid, (r1 - mid.astype(F32)).astype(BF16)]
    cums = _dot(tri, jnp.concatenate(terms, axis=1))

    new_states = []
    for hh in range(heads):
        sl = slice(hh * LANES, (hh + 1) * LANES)
        q, kk, vb = acts[hh]
        c0 = 3 * hh * LANES
        b = cums[:, c0:c0 + LANES] + cums[:, c0 + LANES:c0 + 2 * LANES] + cums[:, c0 + 2 * LANES:c0 + 3 * LANES]

        scores = jnp.where(lvl == 0, _dot_nt(q.astype(BF16), kk.astype(BF16)), 0.0)
        x = b
        blk, level = 1, 1
        while blk < chunk:
            nxt = pltpu.roll(x, chunk - blk, 0)
            qt = (q * jnp.exp(jnp.minimum(b - x, 0.0))).astype(BF16)
            kt = (kk * jnp.exp(jnp.minimum(nxt - b, 0.0))).astype(BF16)
            scores = jnp.where(lvl == level, _dot_nt(qt, kt), scores)
            x = jnp.where((rsub & blk) != 0, pltpu.roll(x, blk, 0), x)
            blk *= 2
            level += 1

        st = st_ref[hh]
        o = _dot(scores.astype(BF16), vb) + _dot_nt((q * jnp.exp(b)).astype(BF16), st.astype(BF16))
        b_end = b[chunk - 1:chunk, :]
        kd = (kk * jnp.exp(b_end - b)).astype(BF16)
        st_new = st * jnp.exp(b_end) + _dot_tn(vb, kd)
        st_ref[hh] = st_new
        new_states.append(st_new)

        o = o[:rows]
        ug = ug_ref[:, sl]
        o = o * lax.rsqrt(jnp.mean(o * o, axis=-1, keepdims=True) + RMS_EPS) * nw_ref[...] * (ug * _sigmoid(ug))
        o_ref[:, sl] = o.astype(o_ref.dtype)

    @pl.when(ci == nchunks - 1)
    def _():
        for hh in range(heads):
            sout_ref[hh] = new_states[hh].T


def hgrn2(u, lb, norm_w, s0, bsz, seq, out_dtype):
    assert A_KDIM == LANES and A_VDIM == LANES
    rows = min(seq, HGRN_CHUNK)
    assert seq % rows == 0
    chunk = max(rows, 2 * SUBLANES)
    nchunks = seq // rows
    lvl = jnp.asarray(_level_matrix(chunk))
    hg = min(HGRN_HEADS, A_HEADS)
    assert A_HEADS % hg == 0
    width = hg * LANES

    def u_spec(group):
        assert IN_OFFS[group] % width == 0
        off = IN_OFFS[group] // width
        return pl.BlockSpec((rows, width), lambda b, hh, c: (b * nchunks + c, off + hh))

    vec_spec = pl.BlockSpec((1, width), lambda b, hh, c: (0, hh))
    state_spec = pl.BlockSpec((None, hg, A_KDIM, A_VDIM), lambda b, hh, c: (b, hh, 0, 0))
    return pl.pallas_call(
        functools.partial(_hgrn_body, rows=rows, chunk=chunk, nchunks=nchunks, heads=hg),
        grid=(bsz, A_HEADS // hg, nchunks),
        in_specs=[u_spec(0), u_spec(1), u_spec(2), u_spec(3), vec_spec,
                  pl.BlockSpec((1, LANES), lambda b, hh, c: (0, 0)),
                  state_spec,
                  pl.BlockSpec((chunk, chunk), lambda b, hh, c: (0, 0))],
        out_specs=(pl.BlockSpec((rows, width), lambda b, hh, c: (b * nchunks + c, hh)), state_spec),
        out_shape=(jax.ShapeDtypeStruct((bsz * seq, A_VW), out_dtype),
                   jax.ShapeDtypeStruct((bsz, A_HEADS, A_KDIM, A_VDIM), F32)),
        scratch_shapes=[pltpu.VMEM((hg, A_VDIM, A_KDIM), F32)],
        compiler_params=_cparams("parallel", "parallel", "arbitrary"),
        name="hgrn2",
    )(u, u, u, u, lb.reshape(1, A_KW), norm_w.reshape(1, A_VDIM), s0, lvl)


def _moba_prompt_body(q_ref, k_ref, v_ref, cb_ref, o_ref, kt_ref, vb_ref, km_ref, s_ref, p_ref, *, seq, nblk):
    kf = k_ref[...]
    kt_ref[...] = kf.T.astype(BF16)
    vb_ref[...] = v_ref[...].astype(BF16)
    km_ref[...] = jnp.zeros_like(km_ref)
    km_ref[0:nblk, :] = jnp.sum(kf.reshape(nblk, MOBA_BLOCK, B_HDIM), axis=1) * (1.0 / MOBA_BLOCK)

    qb, kb = MOBA_QBLOCK, MOBA_BLOCK
    k_sel = min(MOBA_TOPK, nblk - 1)
    c2 = (B_HDIM ** -0.5) * math.log2(math.e)
    lane = lax.broadcasted_iota(jnp.int32, (qb, LANES), 1)

    for qi in range(seq // qb):
        q0 = qi * qb
        own = q0 // kb
        width = (own + 1) * kb
        qs = q_ref[q0:q0 + qb, :]
        qsb = qs.astype(BF16)
        bias = None
        if own > k_sel:
            gate = _dot_nt(qs, km_ref[...], precision=HIGHEST)
            gm = jnp.where(lane < own, gate, NEG_BIG)
            rank = jnp.zeros((qb, LANES), jnp.int32)
            for n in range(own):
                cm = gm[:, n:n + 1]
                ahead = (cm > gm) | ((cm == gm) & (lane > n))
                rank = rank + jnp.where(ahead, 1, 0)
            bias = jnp.where(rank < k_sel, 0.0, NEG_BIG)

        mx = None
        for n in range(own + 1):
            cols = slice(n * kb, (n + 1) * kb)
            sn = _dot(qsb, kt_ref[:, cols])
            if n == own:
                sn = sn + cb_ref[(q0 - own * kb) // qb]
            elif bias is not None:
                sn = sn + bias[:, n:n + 1]
            s_ref[qi % 2, :, cols] = sn
            part = sn[:, :LANES]
            for j in range(1, kb // LANES):
                part = jnp.maximum(part, sn[:, j * LANES:(j + 1) * LANES])
            mx = part if mx is None else jnp.maximum(mx, part)
        mc = jnp.max(mx, axis=-1, keepdims=True) * c2
        ls = None
        for n in range(own + 1):
            cols = slice(n * kb, (n + 1) * kb)
            p = jnp.exp2(s_ref[qi % 2, :, cols] * c2 - mc)
            p_ref[qi % 2, :, cols] = p.astype(BF16)
            for j in range(kb // LANES):
                pj = p[:, j * LANES:(j + 1) * LANES]
                ls = pj if ls is None else ls + pj
        l = jnp.sum(ls, axis=-1, keepdims=True)
        o = _dot(p_ref[qi % 2, :, 0:width], vb_ref[0:width, :]) / l
        o_ref[q0:q0 + qb, :] = o.astype(o_ref.dtype)


def moba_prompt(q, k, v, bsz, seq):
    assert seq % MOBA_BLOCK == 0 and B_HDIM == LANES and seq // MOBA_BLOCK <= LANES
    assert MOBA_BLOCK % MOBA_QBLOCK == 0
    nblk = seq // MOBA_BLOCK
    per = MOBA_BLOCK // MOBA_QBLOCK
    r = np.arange(MOBA_QBLOCK)[None, :, None] + MOBA_QBLOCK * np.arange(per)[:, None, None]
    causal = np.where(np.arange(MOBA_BLOCK)[None, None, :] <= r, 0.0, NEG_BIG).astype(np.float32)
    spec = pl.BlockSpec((seq, B_HDIM), lambda b, h: (b, h))
    return pl.pallas_call(
        functools.partial(_moba_prompt_body, seq=seq, nblk=nblk),
        grid=(bsz, B_HEADS),
        in_specs=[spec, spec, spec,
                  pl.BlockSpec((per, MOBA_QBLOCK, MOBA_BLOCK), lambda b, h: (0, 0, 0))],
        out_specs=spec,
        out_shape=jax.ShapeDtypeStruct((bsz * seq, B_W), BF16),
        scratch_shapes=[pltpu.VMEM((B_HDIM, seq), BF16), pltpu.VMEM((seq, B_HDIM), BF16),
                        pltpu.VMEM((LANES, B_HDIM), F32),
                        pltpu.VMEM((2, MOBA_QBLOCK, seq), F32), pltpu.VMEM((2, MOBA_QBLOCK, seq), BF16)],
        compiler_params=_cparams("parallel", "parallel"),
        name="moba_prompt",
    )(q, k, v, jnp.asarray(causal))


def _moba_select_body(pt_ref, *refs, tq, nblk, npages_step):
    page_refs = refs[:npages_step]
    q_ref, o_ref, km_ref = refs[npages_step:]
    j = pl.program_id(1)
    per_block = MOBA_BLOCK // PAGE_SIZE
    blocks_step = npages_step // per_block
    for i in range(blocks_step):
        acc = jnp.sum(page_refs[i * per_block][...], axis=0)
        for p in range(1, per_block):
            acc = acc + jnp.sum(page_refs[i * per_block + p][...], axis=0)
        row0 = pl.multiple_of((j * blocks_step + i) * B_HEADS, B_HEADS)
        km_ref[pl.ds(row0, B_HEADS), :] = acc * (1.0 / MOBA_BLOCK)

    @pl.when(j == pl.num_programs(1) - 1)
    def _():
        k_sel = min(MOBA_TOPK, nblk)
        lane_g = lax.broadcasted_iota(jnp.int32, (tq, nblk), 1)
        lane_o = lax.broadcasted_iota(jnp.int32, (tq, LANES), 1)
        for h in range(B_HEADS):
            kmh = km_ref[pl.ds(h, nblk, stride=B_HEADS), :]
            g = _dot_nt(q_ref[:, h * B_HDIM:(h + 1) * B_HDIM], kmh, precision=HIGHEST)
            res = jnp.zeros((tq, LANES), jnp.int32)
            for r in range(k_sel):
                mx = jnp.max(g, axis=-1, keepdims=True)
                ix = jnp.min(jnp.where(g == mx, lane_g, nblk), axis=-1, keepdims=True)
                res = jnp.where(lane_o == r, ix, res)
                g = jnp.where(lane_g == ix, -jnp.inf, g)
            o_ref[h * tq:(h + 1) * tq, :] = res


def moba_select(q, cache_k, page_table, tq):
    dbsz, npages = page_table.shape
    nblk = (PAST_LEN // MOBA_BLOCK)
    assert PAST_LEN % MOBA_BLOCK == 0 and npages * PAGE_SIZE == PAST_LEN
    ps = min(SEL_PAGES, npages)
    assert npages % ps == 0 and ps % (MOBA_BLOCK // PAGE_SIZE) == 0

    def page_spec(i):
        return pl.BlockSpec((None, PAGE_SIZE, B_HEADS, B_HDIM),
                            lambda b, j, pt: (pt[b, j * ps + i], 0, 0, 0))

    grid_spec = pltpu.PrefetchScalarGridSpec(
        num_scalar_prefetch=1,
        grid=(dbsz, npages // ps),
        in_specs=[page_spec(i) for i in range(ps)]
        + [pl.BlockSpec((tq, B_W), lambda b, j, pt: (b, 0))],
        out_specs=pl.BlockSpec((None, B_HEADS * tq, LANES), lambda b, j, pt: (b, 0, 0)),
        scratch_shapes=[pltpu.VMEM((nblk * B_HEADS, B_HDIM), F32)],
    )
    return pl.pallas_call(
        functools.partial(_moba_select_body, tq=tq, nblk=nblk, npages_step=ps),
        grid_spec=grid_spec,
        out_shape=jax.ShapeDtypeStruct((dbsz, B_HEADS * tq, LANES), jnp.int32),
        compiler_params=_cparams("parallel", "arbitrary"),
        name="moba_select",
    )(page_table, *([cache_k] * ps), q)


def _moba_decode_body(pt_ref, ix_ref, q_ref, kn_ref, vn_ref, ck_ref, cv_ref, o_ref,
                      kbuf, vbuf, sems, *, tq, k_sel, nsteps):
    b = pl.program_id(0)
    h = pl.program_id(1)
    step = b * B_HEADS + h
    slot = step % 2
    per_block = MOBA_BLOCK // PAGE_SIZE
    nsel = tq * k_sel * MOBA_BLOCK
    own_rows = kbuf.shape[1] - nsel

    def copies(st, sl):
        bb, hh = st // B_HEADS, st % B_HEADS
        out = []
        for t in range(tq):
            for s in range(k_sel):
                blk = ix_ref[(st * tq + t) * k_sel + s]
                for p in range(per_block):
                    page = pt_ref[bb, blk * per_block + p]
                    dst = pl.ds(((t * k_sel + s) * per_block + p) * PAGE_SIZE, PAGE_SIZE)
                    out.append(pltpu.make_async_copy(ck_ref.at[page, :, hh, :], kbuf.at[sl, dst, :], sems.at[0, sl]))
                    out.append(pltpu.make_async_copy(cv_ref.at[page, :, hh, :], vbuf.at[sl, dst, :], sems.at[1, sl]))
        return out

    @pl.when(step == 0)
    def _():
        for cp in copies(step, slot):
            cp.start()

    @pl.when(step + 1 < nsteps)
    def _():
        for cp in copies(step + 1, 1 - slot):
            cp.start()

    zpad = jnp.zeros((own_rows - tq, B_HDIM), F32)
    kbuf[slot, pl.ds(nsel, own_rows), :] = jnp.concatenate([kn_ref[...], zpad], axis=0)
    vbuf[slot, pl.ds(nsel, own_rows), :] = jnp.concatenate([vn_ref[...], zpad], axis=0)
    for cp in copies(step, slot):
        cp.wait()

    ncol = kbuf.shape[1]
    qb = q_ref[...].astype(BF16)
    s = _dot_nt(qb, kbuf[slot].astype(BF16)) * (B_HDIM ** -0.5)
    col = lax.broadcasted_iota(jnp.int32, (tq, ncol), 1)
    row = lax.broadcasted_iota(jnp.int32, (tq, ncol), 0)
    lo = row * (k_sel * MOBA_BLOCK)
    allowed = ((col >= lo) & (col < lo + k_sel * MOBA_BLOCK)) | ((col >= nsel) & (col <= nsel + row))
    s = jnp.where(allowed, s, NEG_BIG)
    m = jnp.max(s, axis=-1, keepdims=True)
    p = jnp.exp(s - m)
    l = jnp.sum(p, axis=-1, keepdims=True)
    o_ref[...] = _dot(p.astype(BF16), vbuf[slot].astype(BF16)) / l


def moba_decode(q, kn, vn, cache_k, cache_v, page_table, sel, tq):
    dbsz = page_table.shape[0]
    k_sel = min(MOBA_TOPK, PAST_LEN // MOBA_BLOCK)
    assert PAST_LEN % MOBA_BLOCK == 0, "cached rows of the own block are not supported"
    assert k_sel > 0 and tq <= LANES
    nrows = tq * k_sel * MOBA_BLOCK + LANES
    spec = pl.BlockSpec((tq, B_HDIM), lambda b, h, pt, ix: (b, h))
    grid_spec = pltpu.PrefetchScalarGridSpec(
        num_scalar_prefetch=2,
        grid=(dbsz, B_HEADS),
        in_specs=[spec, spec, spec,
                  pl.BlockSpec(memory_space=pl.ANY), pl.BlockSpec(memory_space=pl.ANY)],
        out_specs=spec,
        scratch_shapes=[pltpu.VMEM((2, nrows, B_HDIM), F32), pltpu.VMEM((2, nrows, B_HDIM), F32),
                        pltpu.SemaphoreType.DMA((2, 2))],
    )
    return pl.pallas_call(
        functools.partial(_moba_decode_body, tq=tq, k_sel=k_sel, nsteps=dbsz * B_HEADS),
        grid_spec=grid_spec,
        out_shape=jax.ShapeDtypeStruct((dbsz * tq, B_W), F32),
        compiler_params=_cparams("arbitrary", "arbitrary"),
        name="moba_decode",
    )(page_table, sel, q, kn, vn, cache_k, cache_v)


def _merge_body(oa_ref, ob_ref, ga_ref, gb_ref, wa_ref, wb_ref, o_ref):
    a = _dot(oa_ref[...].astype(BF16), wa_ref[...])
    b = _dot(ob_ref[...].astype(BF16), wb_ref[...])
    o_ref[...] = (_sigmoid(ga_ref[...]) * a + _sigmoid(gb_ref[...]) * b).astype(o_ref.dtype)


def merge_branches(o_a, o_b, u, w_a, w_b, bm):
    n = o_a.shape[0]
    bn = min(MM_COLS, D_MODEL)
    assert IN_OFFS[7] % bn == 0 and IN_OFFS[8] % bn == 0 and D_MODEL % bn == 0
    ca, cb = IN_OFFS[7] // bn, IN_OFFS[8] // bn
    return pl.pallas_call(
        _merge_body,
        grid=(D_MODEL // bn, n // bm),
        in_specs=[pl.BlockSpec((bm, A_VW), lambda j, i: (i, 0)),
                  pl.BlockSpec((bm, B_W), lambda j, i: (i, 0)),
                  pl.BlockSpec((bm, bn), lambda j, i: (i, ca + j)),
                  pl.BlockSpec((bm, bn), lambda j, i: (i, cb + j)),
                  pl.BlockSpec((A_VW, bn), lambda j, i: (0, j)),
                  pl.BlockSpec((B_W, bn), lambda j, i: (0, j))],
        out_specs=pl.BlockSpec((bm, bn), lambda j, i: (i, j)),
        out_shape=jax.ShapeDtypeStruct((n, D_MODEL), BF16),
        compiler_params=_cparams("parallel", "parallel"),
        name="merge_branches",
    )(o_a, o_b, u, u, w_a, w_b)


def _pack_words(h):
    half = h.shape[1] // 2
    lo = pltpu.bitcast(h[:, :half].astype(BF16).astype(F32), jnp.uint32)
    hi = pltpu.bitcast(h[:, half:].astype(BF16).astype(F32), jnp.uint32)
    return (lo >> 16) | (hi & jnp.uint32(0xFFFF0000))


def _router_body(xp_ref, xs_ref, nw_ref, wr_ref, br_ref, hp_ref, idx_ref, gate_ref, *, n_prompt_tiles):
    i = pl.program_id(0)
    x1 = jnp.where(i < n_prompt_tiles, xp_ref[...], xs_ref[...])
    h2 = _rms(x1, nw_ref[...])
    words = _pack_words(h2)
    for s in range(WORDS_PER_ROW):
        hp_ref[s] = words[:, s * LANES:(s + 1) * LANES]
    h_hi = h2.astype(BF16)
    h_lo = (h2 - h_hi.astype(F32)).astype(BF16)
    a = _dot(h_hi, wr_ref[...])
    b = _dot(h_lo, wr_ref[...])
    logits = a + pltpu.roll(a, LANES - N_EXPERTS, 1) + b + br_ref[...]
    lane = lax.broadcasted_iota(jnp.int32, logits.shape, 1)
    g = jnp.where(lane < N_EXPERTS, logits, -jnp.inf)
    idx = jnp.zeros(logits.shape, jnp.int32)
    val = jnp.zeros(logits.shape, F32)
    top = None
    for r in range(TOP_K):
        mx = jnp.max(g, axis=-1, keepdims=True)
        ix = jnp.min(jnp.where(g == mx, lane, LANES), axis=-1, keepdims=True)
        top = mx if top is None else top
        idx = jnp.where(lane == r, ix, idx)
        val = jnp.where(lane == r, jnp.exp(mx - top), val)
        g = jnp.where(lane == ix, -jnp.inf, g)
    idx_ref[...] = idx
    gate_ref[...] = val / jnp.sum(val, axis=-1, keepdims=True)


def _dual_maps(tp):
    return (lambda i: (jnp.minimum(i, tp - 1), 0)), (lambda i: (jnp.maximum(i - tp, 0), 0))


def moe_router(x1_p, x1_s, norm_w, w_router, b_router):
    bm = ROW_TILE
    n_p, n_s = x1_p.shape[0], x1_s.shape[0]
    assert n_p % bm == 0 and n_s % bm == 0 and 2 * N_EXPERTS <= LANES and TOP_K <= LANES
    tp, ts = n_p // bm, n_s // bm
    n = n_p + n_s
    w_hi = w_router.astype(BF16)
    w_lo = (w_router - w_hi.astype(F32)).astype(BF16)
    wr = jnp.zeros((D_MODEL, LANES), BF16).at[:, :N_EXPERTS].set(w_hi).at[:, N_EXPERTS:2 * N_EXPERTS].set(w_lo)
    br = jnp.zeros((1, LANES), F32).at[0, :N_EXPERTS].set(b_router)
    p_map, s_map = _dual_maps(tp)
    full = lambda i: (0, 0)
    row = lambda i: (i, 0)
    return pl.pallas_call(
        functools.partial(_router_body, n_prompt_tiles=tp),
        grid=(tp + ts,),
        in_specs=[pl.BlockSpec((bm, D_MODEL), p_map), pl.BlockSpec((bm, D_MODEL), s_map),
                  pl.BlockSpec((1, D_MODEL), full),
                  pl.BlockSpec((D_MODEL, LANES), full), pl.BlockSpec((1, LANES), full)],
        out_specs=(pl.BlockSpec((WORDS_PER_ROW, bm, LANES), lambda i: (0, i, 0)),
                   pl.BlockSpec((bm, LANES), row), pl.BlockSpec((bm, LANES), row)),
        out_shape=(jax.ShapeDtypeStruct((WORDS_PER_ROW, n, LANES), jnp.uint32),
                   jax.ShapeDtypeStruct((n, LANES), jnp.int32),
                   jax.ShapeDtypeStruct((n, LANES), F32)),
        compiler_params=_cparams("parallel"),
        name="moe_router",
    )(x1_p, x1_s, norm_w.reshape(1, D_MODEL), wr, br)


def _rank_body(idx_ref, rank_ref, cnt_ref, carry_ref):
    i = pl.program_id(0)

    @pl.when(i == 0)
    def _():
        carry_ref[...] = jnp.zeros_like(carry_ref)

    idx = idx_ref[...]
    bt = idx.shape[0]
    lane = lax.broadcasted_iota(jnp.int32, (bt, LANES), 1)
    rowi = lax.broadcasted_iota(jnp.int32, (bt, bt), 0)
    coli = lax.broadcasted_iota(jnp.int32, (bt, bt), 1)
    before = jnp.where(rowi > coli, 1.0, 0.0).astype(BF16)
    base = carry_ref[...]
    res = jnp.zeros((bt, LANES), jnp.int32)
    for k in range(TOP_K):
        onehot = jnp.where(lane == idx[:, k:k + 1], 1.0, 0.0)
        earlier = _dot(before, onehot.astype(BF16))
        rk = jnp.sum(onehot * (earlier + base), axis=-1, keepdims=True)
        res = jnp.where(lane == k, rk.astype(jnp.int32), res)
        base = base + jnp.sum(onehot, axis=0, keepdims=True)
    rank_ref[...] = res
    carry_ref[...] = base
    cnt_ref[...] = jnp.broadcast_to(base, cnt_ref.shape).astype(jnp.int32)


def moe_rank(idx):
    n = idx.shape[0]
    bt = ROW_TILE
    return pl.pallas_call(
        _rank_body,
        grid=(n // bt,),
        in_specs=[pl.BlockSpec((bt, LANES), lambda i: (i, 0))],
        out_specs=(pl.BlockSpec((bt, LANES), lambda i: (i, 0)),
                   pl.BlockSpec((SUBLANES, LANES), lambda i: (0, 0))),
        out_shape=(jax.ShapeDtypeStruct((n, LANES), jnp.int32),
                   jax.ShapeDtypeStruct((SUBLANES, LANES), jnp.int32)),
        scratch_shapes=[pltpu.VMEM((1, LANES), F32)],
        compiler_params=_cparams("arbitrary"),
        name="moe_rank",
    )(idx)


def _load_slots(dest_ref, dsm, sem_idx):
    cp = pltpu.make_async_copy(dest_ref.at[pl.program_id(0)], dsm, sem_idx)
    cp.start()
    cp.wait()


def _dispatch_body(pe_ref, nu_ref, hp_ref, dest_ref, xs_ref, dsm, zbuf, sem_idx, sem, zsem, *, bt, n_blocks):
    wpr = WORDS_PER_ROW

    @pl.when(pl.program_id(0) == 0)
    def _():
        zbuf[...] = jnp.zeros_like(zbuf)

        def clear(row0):
            return pltpu.make_async_copy(
                zbuf, xs_ref.at[pl.ds(pl.multiple_of(row0 * wpr, wpr), MOE_ROWS * wpr), :], zsem)

        def seg_tail(e):
            return jnp.maximum(pe_ref[e] - MOE_ROWS, 0)

        for e in range(N_EXPERTS):
            clear(seg_tail(e)).start()
        lax.fori_loop(nu_ref[0], n_blocks, lambda b, c: (clear(b * MOE_ROWS).start(), c)[1], 0)
        for e in range(N_EXPERTS):
            clear(seg_tail(e)).wait()
        lax.fori_loop(nu_ref[0], n_blocks, lambda b, c: (clear(b * MOE_ROWS).wait(), c)[1], 0)

    _load_slots(dest_ref, dsm, sem_idx)

    def issue(r, c):
        for k in range(TOP_K):
            d = dsm[r * TOP_K + k]
            pltpu.make_async_copy(hp_ref.at[:, r, :],
                                  xs_ref.at[pl.ds(pl.multiple_of(d * wpr, wpr), wpr), :], sem).start()
        return c

    lax.fori_loop(0, bt, issue, 0, unroll=DMA_UNROLL)
    for k in range(TOP_K):
        whole = xs_ref.at[pl.ds(0, bt * wpr), :]
        pltpu.make_async_copy(whole, whole, sem).wait()


def moe_dispatch(hp, dest, pends, n_used, n_blocks):
    bt = ROW_TILE
    nt = dest.shape[0]
    wpr = WORDS_PER_ROW
    grid_spec = pltpu.PrefetchScalarGridSpec(
        num_scalar_prefetch=2,
        grid=(nt,),
        in_specs=[pl.BlockSpec((wpr, bt, LANES), lambda i, pe, nu: (0, i, 0)),
                  pl.BlockSpec(memory_space=pl.ANY)],
        out_specs=pl.BlockSpec(memory_space=pl.ANY),
        scratch_shapes=[pltpu.SMEM((bt * TOP_K,), jnp.int32),
                        pltpu.VMEM((MOE_ROWS * wpr, LANES), jnp.uint32),
                        pltpu.SemaphoreType.DMA, pltpu.SemaphoreType.DMA, pltpu.SemaphoreType.DMA],
    )
    return pl.pallas_call(
        functools.partial(_dispatch_body, bt=bt, n_blocks=n_blocks),
        grid_spec=grid_spec,
        out_shape=jax.ShapeDtypeStruct((n_blocks * MOE_ROWS * wpr, LANES), jnp.uint32),
        compiler_params=_cparams("arbitrary"),
        name="moe_dispatch",
    )(pends, n_used, hp, dest)


def _experts_body(be_ref, nu_ref, x_ref, w1g_ref, w1l_ref, b1g_ref, b1l_ref, w2_ref, b2_ref,
                  o_ref, xb_ref, acc_ref, *, rows):
    i = pl.program_id(0)
    c = pl.program_id(1)
    wpr = WORDS_PER_ROW

    @pl.when(i < nu_ref[0])
    def _():
        @pl.when(c == 0)
        def _():
            half = wpr * LANES
            for s in range(wpr):
                w = x_ref[pl.ds(s, rows, stride=wpr), :]
                lo = pltpu.bitcast(w << 16, F32)
                hi = pltpu.bitcast(w & jnp.uint32(0xFFFF0000), F32)
                xb_ref[:, s * LANES:(s + 1) * LANES] = lo.astype(BF16)
                xb_ref[:, half + s * LANES:half + (s + 1) * LANES] = hi.astype(BF16)
            acc_ref[...] = jnp.broadcast_to(b2_ref[...], acc_ref.shape)

        xb = xb_ref[...]
        gu = _dot(xb, w1g_ref[...]) + b1g_ref[...]
        li = _dot(xb, w1l_ref[...]) + b1l_ref[...]
        glu = jnp.minimum(gu, SWIGLU_LIMIT)
        lin = jnp.clip(li, -SWIGLU_LIMIT, SWIGLU_LIMIT)
        act = glu * _sigmoid(SWIGLU_ALPHA * glu) * (lin + 1.0)
        acc_ref[...] += _dot(act.astype(BF16), w2_ref[...])

        @pl.when(c == pl.num_programs(1) - 1)
        def _():
            words = _pack_words(acc_ref[...])
            for s in range(wpr):
                o_ref[pl.ds(s, rows, stride=wpr), :] = words[:, s * LANES:(s + 1) * LANES]

    @pl.when((i >= nu_ref[0]) & (c == 0))
    def _():
        o_ref[...] = jnp.zeros_like(o_ref)


def moe_experts(xs, blk_exp, n_used, w1, b1, w2, b2):
    rows, ff = MOE_ROWS, min(MOE_FF, D_FF)
    wpr = WORDS_PER_ROW
    n_rows = xs.shape[0] // wpr
    nb = n_rows // rows
    nc = D_FF // ff
    lin_off = D_FF // ff

    def eff(i, c, be, nu):
        live = i < nu[0]
        return jnp.where(live, i, nu[0] - 1), jnp.where(live, c, nc - 1)

    def x_map(i, c, be, nu):
        return (eff(i, c, be, nu)[0], 0)

    def w1g_map(i, c, be, nu):
        ii, cc = eff(i, c, be, nu)
        return (be[ii], 0, cc)

    def w1l_map(i, c, be, nu):
        ii, cc = eff(i, c, be, nu)
        return (be[ii], 0, lin_off + cc)

    def w2_map(i, c, be, nu):
        ii, cc = eff(i, c, be, nu)
        return (be[ii], cc, 0)

    def b2_map(i, c, be, nu):
        return (be[eff(i, c, be, nu)[0]], 0, 0)

    grid_spec = pltpu.PrefetchScalarGridSpec(
        num_scalar_prefetch=2,
        grid=(nb, nc),
        in_specs=[pl.BlockSpec((rows * wpr, LANES), x_map),
                  pl.BlockSpec((None, D_MODEL, ff), w1g_map),
                  pl.BlockSpec((None, D_MODEL, ff), w1l_map),
                  pl.BlockSpec((None, 1, ff), w1g_map),
                  pl.BlockSpec((None, 1, ff), w1l_map),
                  pl.BlockSpec((None, ff, D_MODEL), w2_map),
                  pl.BlockSpec((None, 1, D_MODEL), b2_map)],
        out_specs=pl.BlockSpec((rows * wpr, LANES), lambda i, c, be, nu: (i, 0)),
        scratch_shapes=[pltpu.VMEM((rows, D_MODEL), BF16), pltpu.VMEM((rows, D_MODEL), F32)],
    )
    return pl.pallas_call(
        functools.partial(_experts_body, rows=rows),
        grid_spec=grid_spec,
        out_shape=jax.ShapeDtypeStruct((n_rows * wpr, LANES), jnp.uint32),
        compiler_params=_cparams("arbitrary", "arbitrary"),
        name="moe_experts",
    )(blk_exp, n_used, xs, w1, w1, b1, b1, w2, b2)


def _tail_body(xp_ref, xs_ref, gate_ref, pp_ref, ps_ref, nw_ref, wg_ref, wp_ref, nf_ref, dest_ref, y_ref,
               op_ref, os_ref, buf, dsm, sem_idx, sems, *, bt, n_prompt_tiles, n_tiles):
    i = pl.program_id(0)
    slot = i % 2
    wpr = WORDS_PER_ROW
    nslots = bt * TOP_K

    def fetch(tile, sl):
        cp = pltpu.make_async_copy(dest_ref.at[tile], dsm.at[pl.ds(sl * nslots, nslots)], sem_idx)
        cp.start()
        cp.wait()

        def issue(r, c):
            for k in range(TOP_K):
                d = dsm[sl * nslots + r * TOP_K + k]
                pltpu.make_async_copy(y_ref.at[pl.ds(pl.multiple_of(d * wpr, wpr), wpr), :],
                                      buf.at[sl, k, :, r, :], sems.at[sl]).start()
            return c

        lax.fori_loop(0, bt, issue, 0, unroll=DMA_UNROLL)

    @pl.when(i == 0)
    def _():
        fetch(0, 0)

    @pl.when(i + 1 < n_tiles)
    def _():
        fetch(i + 1, 1 - slot)

    for k in range(TOP_K):
        whole = y_ref.at[pl.ds(0, bt * wpr), :]
        pltpu.make_async_copy(whole, whole, sems.at[slot]).wait()

    first = i < n_prompt_tiles
    gates = gate_ref[...]
    gk = [jnp.broadcast_to(gates[:, k:k + 1], (bt, LANES)) for k in range(TOP_K)]
    lo_cols, hi_cols = [], []
    for s in range(wpr):
        lo = hi = None
        for k in range(TOP_K):
            w = buf[slot, k, s]
            wl = gk[k] * pltpu.bitcast(w << 16, F32)
            wh = gk[k] * pltpu.bitcast(w & jnp.uint32(0xFFFF0000), F32)
            lo = wl if lo is None else lo + wl
            hi = wh if hi is None else hi + wh
        lo_cols.append(lo)
        hi_cols.append(hi)
    x = jnp.where(first, xp_ref[...], xs_ref[...]) + jnp.concatenate(lo_cols + hi_cols, axis=1)
    p = jnp.where(first, pp_ref[...], ps_ref[...])
    gate = _sigmoid(_dot(_rms(x, nw_ref[...]).astype(BF16), wg_ref[...]))
    x = x + gate * _dot(p.astype(BF16), wp_ref[...])
    out = _rms(x, nf_ref[...])

    @pl.when(first)
    def _():
        op_ref[...] = out

    @pl.when(jnp.logical_not(first))
    def _():
        os_ref[...] = out


def moe_combine_final(x1_p, x1_s, gates, dest, y, p_p, p_s, norm_ple, w_gate, w_ple, norm_final):
    bt = ROW_TILE
    n_p, n_s = x1_p.shape[0], x1_s.shape[0]
    tp, ts = n_p // bt, n_s // bt
    p_map, s_map = _dual_maps(tp)
    full = lambda i: (0, 0)
    return pl.pallas_call(
        functools.partial(_tail_body, bt=bt, n_prompt_tiles=tp, n_tiles=tp + ts),
        grid=(tp + ts,),
        in_specs=[pl.BlockSpec((bt, D_MODEL), p_map), pl.BlockSpec((bt, D_MODEL), s_map),
                  pl.BlockSpec((bt, LANES), lambda i: (i, 0)),
                  pl.BlockSpec((bt, PLE_DIM), p_map), pl.BlockSpec((bt, PLE_DIM), s_map),
                  pl.BlockSpec((1, D_MODEL), full), pl.BlockSpec((D_MODEL, D_MODEL), full),
                  pl.BlockSpec((PLE_DIM, D_MODEL), full), pl.BlockSpec((1, D_MODEL), full),
                  pl.BlockSpec(memory_space=pl.ANY),
                  pl.BlockSpec(memory_space=pl.ANY)],
        out_specs=(pl.BlockSpec((bt, D_MODEL), p_map), pl.BlockSpec((bt, D_MODEL), s_map)),
        out_shape=(jax.ShapeDtypeStruct((n_p, D_MODEL), F32), jax.ShapeDtypeStruct((n_s, D_MODEL), F32)),
        scratch_shapes=[pltpu.VMEM((2, TOP_K, WORDS_PER_ROW, bt, LANES), jnp.uint32),
                        pltpu.SMEM((2 * bt * TOP_K,), jnp.int32),
                        pltpu.SemaphoreType.DMA, pltpu.SemaphoreType.DMA((2,))],
        compiler_params=_cparams("arbitrary"),
        name="moe_combine_final",
    )(x1_p, x1_s, gates, p_p, p_s, norm_ple.reshape(1, D_MODEL), w_gate, w_ple,
      norm_final.reshape(1, D_MODEL), dest, y)


def _moe_ffn_tail(x1_p, x1_s, norm_w, w_router, b_router, w1, b1, w2, b2, tail_args):
    hp, idx, gates = moe_router(x1_p, x1_s, norm_w, w_router, b_router)
    n = idx.shape[0]
    bt = ROW_TILE
    rank, counts = moe_rank(idx)
    counts = counts[0, :N_EXPERTS]
    pcounts = (counts + MOE_ROWS - 1) // MOE_ROWS * MOE_ROWS
    pends = jnp.cumsum(pcounts)
    pstarts = pends - pcounts
    e_tok = idx[:, :TOP_K]
    dest = (pstarts[e_tok] + rank[:, :TOP_K]).astype(jnp.int32).reshape(n // bt, bt * TOP_K)
    n_blocks = -(-(n * TOP_K + N_EXPERTS * (MOE_ROWS - 1)) // MOE_ROWS)
    blk_start = jnp.arange(n_blocks, dtype=jnp.int32) * MOE_ROWS
    blk_exp = jnp.minimum(jnp.sum((pends[None, :] <= blk_start[:, None]).astype(jnp.int32), axis=1),
                          N_EXPERTS - 1).astype(jnp.int32)
    n_used = (pends[-1] // MOE_ROWS).astype(jnp.int32).reshape(1)
    xs = moe_dispatch(hp, dest, pends.astype(jnp.int32), n_used, n_blocks)
    y = moe_experts(xs, blk_exp, n_used, w1, b1, w2, b2)
    return moe_combine_final(x1_p, x1_s, gates, dest, y, *tail_args)


def _row_tile(n, pref):
    return pref if n % pref == 0 else ROW_TILE


def kernel(x_prompt, x_sample, cache_k, cache_v, state_hgrn, page_table, p_prompt, p_sample, hgrn_lb, norm_mix, w_in, hgrn_norm, w_a, w_b, w_o, norm_moe, w_router, b_router, w_moe1, b_moe1, w_moe2, b_moe2, norm_ple, w_ple, w_ple_gate, norm_final):
    assert DEPTH == 1
    bsz, seq, _ = x_prompt.shape
    dbsz, tq, _ = x_sample.shape
    n_p, n_s = bsz * seq, dbsz * tq
    l = 0
    lb = jnp.cumsum(jax.nn.softmax(hgrn_lb.astype(F32), axis=0), axis=0)[l]
    w_in_b = w_in[l].astype(BF16)
    w_a_b, w_b_b, w_o_b = w_a[l].astype(BF16), w_b[l].astype(BF16), w_o[l].astype(BF16)
    w1_b, w2_b = w_moe1[l].astype(BF16), w_moe2[l].astype(BF16)
    b1 = b_moe1[l].reshape(N_EXPERTS, 1, 2 * D_FF)
    b2 = b_moe2[l].reshape(N_EXPERTS, 1, D_MODEL)
    wg_b, wp_b = w_ple_gate[l].astype(BF16), w_ple[l].astype(BF16)
    bn = MM_COLS if IN_DIM % MM_COLS == 0 and D_MODEL % MM_COLS == 0 else B_W

    xp = x_prompt.reshape(n_p, D_MODEL)
    xs = x_sample.reshape(n_s, D_MODEL)

    def mixer_inputs(x, n):
        h = rmsnorm_cast(x, norm_mix[l], _row_tile(n, 512))
        return matmul(h, w_in_b, _row_tile(n, MM_ROWS), bn)

    u_p = mixer_inputs(xp, n_p)
    oa_p, st_p = hgrn2(u_p, lb, hgrn_norm[l], jnp.zeros((bsz, A_HEADS, A_KDIM, A_VDIM), F32), bsz, seq, BF16)
    rt = _row_tile(seq, 512)
    q_p, k_p, v_p = rope_qkv(u_p, _rope_tables(jnp.arange(seq)), rt if seq % rt == 0 else seq)
    ob_p = moba_prompt(q_p, k_p, v_p, bsz, seq)
    m_p = merge_branches(oa_p, ob_p, u_p, w_a_b, w_b_b, _row_tile(n_p, MM_ROWS))
    x1_p = matmul(m_p, w_o_b, _row_tile(n_p, MM_ROWS), bn, residual=xp)

    u_s = mixer_inputs(xs, n_s)
    oa_s, st_s = hgrn2(u_s, lb, hgrn_norm[l], state_hgrn[l], dbsz, tq, F32)
    tabs_s = tuple(jnp.tile(t, (dbsz, 1)) for t in _rope_tables(PAST_LEN + jnp.arange(tq)))
    q_s, k_s, v_s = rope_qkv(u_s, tabs_s, n_s)
    k_sel = min(MOBA_TOPK, PAST_LEN // MOBA_BLOCK)
    sel = moba_select(q_s, cache_k[l], page_table, tq)[:, :, :k_sel].reshape(-1)
    ob_s = moba_decode(q_s, k_s, v_s, cache_k[l], cache_v[l], page_table, sel, tq)
    m_s = merge_branches(oa_s, ob_s, u_s, w_a_b, w_b_b, _row_tile(n_s, MM_ROWS))
    x1_s = matmul(m_s, w_o_b, _row_tile(n_s, MM_ROWS), bn, residual=xs)

    tail_args = (p_prompt[l].reshape(n_p, PLE_DIM), p_sample[l].reshape(n_s, PLE_DIM),
                 norm_ple[l], wg_b, wp_b, norm_final)
    y_p, y_s = _moe_ffn_tail(x1_p, x1_s, norm_moe[l], w_router[l], b_router[l], w1_b, b1, w2_b, b2, tail_args)

    hs_p = (1, bsz, seq, B_HEADS, B_HDIM)
    hs_s = (1, dbsz, tq, B_HEADS, B_HDIM)
    return (y_p.reshape(bsz, seq, D_MODEL), y_s.reshape(dbsz, tq, D_MODEL),
            st_p[None], k_p.reshape(hs_p), v_p.reshape(hs_p),
            st_s[None], k_s.reshape(hs_s), v_s.reshape(hs_s))
```

```python
import functools
import math

import numpy as np
import jax
import jax.numpy as jnp
from jax import lax
from jax.experimental import pallas as pl
from jax.experimental.pallas import tpu as pltpu

D_MODEL = 2048
DEPTH = 1
PAST_LEN = 16384
PAGE_SIZE = 128
A_HEADS = 8
A_KDIM = 128
A_VDIM = 128
B_HEADS = 8
B_HDIM = 128
MOBA_BLOCK = 256
MOBA_TOPK = 3
MOBA_QBLOCK = 128
ROPE_THETA = 500000.0
ROT_DIM = B_HDIM // 4
N_EXPERTS = 32
TOP_K = 4
D_FF = D_MODEL
SWIGLU_LIMIT = 7.0
SWIGLU_ALPHA = 1.702
PLE_DIM = 256
RMS_EPS = 1e-6
NEG_BIG = -1e30

A_KW = A_HEADS * A_KDIM
A_VW = A_HEADS * A_VDIM
B_W = B_HEADS * B_HDIM
IN_WIDTHS = (A_KW, A_KW, A_VW, A_VW, B_W, B_W, B_W, D_MODEL, D_MODEL)
IN_DIM = sum(IN_WIDTHS)
IN_OFFS = tuple(int(v) for v in np.cumsum((0,) + IN_WIDTHS[:-1]))

LANES = 128
SUBLANES = 8
VMEM_LIMIT = 56 * 1024 * 1024

ROW_TILE = 256
MM_ROWS = 1024
MM_COLS = 1024
HGRN_CHUNK = 128
HGRN_HEADS = 8
MOE_ROWS = 512
MOE_FF = 1024
SEL_PAGES = 16
DMA_UNROLL = 8

F32 = jnp.float32
BF16 = jnp.bfloat16
HIGHEST = lax.Precision.HIGHEST
WORDS_PER_ROW = D_MODEL // 2 // LANES


def _cparams(*sem):
    return pltpu.CompilerParams(dimension_semantics=sem, vmem_limit_bytes=VMEM_LIMIT)


def _dot(a, b, **kw):
    return jnp.dot(a, b, preferred_element_type=F32, **kw)


def _dot_nt(a, b, **kw):
    return lax.dot_general(a, b, (((1,), (1,)), ((), ())), preferred_element_type=F32, **kw)


def _dot_tn(a, b, **kw):
    return lax.dot_general(a, b, (((0,), (0,)), ((), ())), preferred_element_type=F32, **kw)


def _sigmoid(x):
    return 1.0 / (1.0 + jnp.exp(-x))


def _rms(x, w):
    ms = jnp.mean(x * x, axis=-1, keepdims=True)
    return x * lax.rsqrt(ms + RMS_EPS) * w


def _rmsnorm_body(x_ref, w_ref, o_ref):
    o_ref[...] = _rms(x_ref[...], w_ref[...]).astype(o_ref.dtype)


def rmsnorm_cast(x, w, bm):
    n, d = x.shape
    return pl.pallas_call(
        _rmsnorm_body,
        grid=(n // bm,),
        in_specs=[pl.BlockSpec((bm, d), lambda i: (i, 0)),
                  pl.BlockSpec((1, d), lambda i: (0, 0))],
        out_specs=pl.BlockSpec((bm, d), lambda i: (i, 0)),
        out_shape=jax.ShapeDtypeStruct((n, d), BF16),
        compiler_params=_cparams("parallel"),
        name="rmsnorm_cast",
    )(x, w.reshape(1, d))


def _matmul_body(a_ref, w_ref, o_ref):
    o_ref[...] = _dot(a_ref[...], w_ref[...]).astype(o_ref.dtype)


def _matmul_res_body(a_ref, w_ref, x_ref, o_ref):
    o_ref[...] = x_ref[...] + _dot(a_ref[...], w_ref[...])


def matmul(a, w, bm, bn, residual=None):
    n, k = a.shape
    m = w.shape[1]
    in_specs = [pl.BlockSpec((bm, k), lambda j, i: (i, 0)),
                pl.BlockSpec((k, bn), lambda j, i: (0, j))]
    args = (a, w)
    if residual is not None:
        in_specs.append(pl.BlockSpec((bm, bn), lambda j, i: (i, j)))
        args = (a, w, residual)
    return pl.pallas_call(
        _matmul_body if residual is None else _matmul_res_body,
        grid=(m // bn, n // bm),
        in_specs=in_specs,
        out_specs=pl.BlockSpec((bm, bn), lambda j, i: (i, j)),
        out_shape=jax.ShapeDtypeStruct((n, m), F32),
        compiler_params=_cparams("parallel", "parallel"),
        name="in_proj" if residual is None else "out_proj",
    )(*args)


def _rope_tables(pos):
    half = ROT_DIM // 2
    inv_freq = jnp.power(ROPE_THETA, -jnp.arange(half, dtype=F32) * (2.0 / ROT_DIM))
    ang = pos.astype(F32)[:, None] * inv_freq[None, :]
    cos, sin = jnp.cos(ang), jnp.sin(ang)
    n = pos.shape[0]
    ones = jnp.ones((n, B_HDIM - ROT_DIM), F32)
    zeros = jnp.zeros((n, B_HDIM - ROT_DIM), F32)
    zh = jnp.zeros((n, half), F32)
    c = jnp.concatenate([cos, cos, ones], axis=1)
    s_up = jnp.concatenate([-sin, zh, zeros], axis=1)
    s_dn = jnp.concatenate([zh, sin, zeros], axis=1)
    return c, s_up, s_dn


def _rope_body(q_ref, k_ref, v_ref, c_ref, su_ref, sd_ref, qo_ref, ko_ref, vo_ref):
    c, su, sd = c_ref[...], su_ref[...], sd_ref[...]
    half = ROT_DIM // 2
    for h in range(B_HEADS):
        sl = slice(h * B_HDIM, (h + 1) * B_HDIM)
        for src, dst in ((q_ref, qo_ref), (k_ref, ko_ref)):
            x = src[:, sl]
            y = x * c + pltpu.roll(x, B_HDIM - half, 1) * su + pltpu.roll(x, half, 1) * sd
            dst[:, sl] = y
    vo_ref[...] = v_ref[...]


def rope_qkv(u, tabs, bm):
    n = u.shape[0]
    c, su, sd = tabs
    tb = c.shape[0] // bm
    cq, ck, cv = (IN_OFFS[4] // B_W, IN_OFFS[5] // B_W, IN_OFFS[6] // B_W)
    tab_spec = pl.BlockSpec((bm, B_HDIM), lambda i: (i % tb, 0))
    out_spec = pl.BlockSpec((bm, B_W), lambda i: (i, 0))
    out_sds = jax.ShapeDtypeStruct((n, B_W), F32)
    return pl.pallas_call(
        _rope_body,
        grid=(n // bm,),
        in_specs=[pl.BlockSpec((bm, B_W), lambda i: (i, cq)),
                  pl.BlockSpec((bm, B_W), lambda i: (i, ck)),
                  pl.BlockSpec((bm, B_W), lambda i: (i, cv)),
                  tab_spec, tab_spec, tab_spec],
        out_specs=(out_spec, out_spec, out_spec),
        out_shape=(out_sds, out_sds, out_sds),
        compiler_params=_cparams("parallel"),
        name="rope_qkv",
    )(u, u, u, c, su, sd)


def _level_matrix(c):
    t = np.arange(c)[:, None]
    s = np.arange(c)[None, :]
    x = np.bitwise_xor(t, s)
    lvl = np.where(x > 0, np.floor(np.log2(np.maximum(x, 1))).astype(np.int32) + 1, 0)
    return np.where(s > t, -1, lvl).astype(np.int32)


def _hgrn_body(uq_ref, uf_ref, ui_ref, ug_ref, lb_ref, nw_ref, s0_ref, lvl_ref,
               o_ref, sout_ref, st_ref, *, rows, chunk, nchunks, heads):
    ci = pl.program_id(2)

    def padded(x, fill):
        if rows == chunk:
            return x
        return jnp.concatenate([x, jnp.full((chunk - rows, x.shape[1]), fill, x.dtype)], axis=0)

    rowi = lax.broadcasted_iota(jnp.int32, (chunk, chunk), 0)
    coli = lax.broadcasted_iota(jnp.int32, (chunk, chunk), 1)
    tri = jnp.where(rowi >= coli, 1.0, 0.0).astype(BF16)
    rsub = lax.broadcasted_iota(jnp.int32, (chunk, A_KDIM), 0)
    lvl = lvl_ref[...]

    @pl.when(ci == 0)
    def _():
        for hh in range(heads):
            st_ref[hh] = s0_ref[hh].T

    acts, terms = [], []
    for hh in range(heads):
        sl = slice(hh * LANES, (hh + 1) * LANES)
        lb = lb_ref[:, sl]
        uq = uq_ref[:, sl]
        q = padded(uq * _sigmoid(uq) * (A_KDIM ** -0.5), 0.0)
        f = padded(lb + (1.0 - lb) * _sigmoid(uf_ref[:, sl]), 1.0)
        g = jnp.log(f)
        acts.append((q, 1.0 - f, padded(ui_ref[:, sl], 0.0).astype(BF16)))
        hi = g.astype(BF16)
        r1 = g - hi.astype(F32)
        mid = r1.astype(BF16)
        terms += [hi, mid, (r1 - mid.astype(F32)).astype(BF16)]
    cums = _dot(tri, jnp.concatenate(terms, axis=1))

    new_states = []
    for hh in range(heads):
        sl = slice(hh * LANES, (hh + 1) * LANES)
        q, kk, vb = acts[hh]
        c0 = 3 * hh * LANES
        b = cums[:, c0:c0 + LANES] + cums[:, c0 + LANES:c0 + 2 * LANES] + cums[:, c0 + 2 * LANES:c0 + 3 * LANES]

        scores = jnp.where(lvl == 0, _dot_nt(q.astype(BF16), kk.astype(BF16)), 0.0)
        x = b
        blk, level = 1, 1
        while blk < chunk:
            nxt = pltpu.roll(x, chunk - blk, 0)
            qt = (q * jnp.exp(jnp.minimum(b - x, 0.0))).astype(BF16)
            kt = (kk * jnp.exp(jnp.minimum(nxt - b, 0.0))).astype(BF16)
            scores = jnp.where(lvl == level, _dot_nt(qt, kt), scores)
            x = jnp.where((rsub & blk) != 0, pltpu.roll(x, blk, 0), x)
            blk *= 2
            level += 1

        st = st_ref[hh]
        o = _dot(scores.astype(BF16), vb) + _dot_nt((q * jnp.exp(b)).astype(BF16), st.astype(BF16))
        b_end = b[chunk - 1:chunk, :]
        kd = (kk * jnp.exp(b_end - b)).astype(BF16)
        st_new = st * jnp.exp(b_end) + _dot_tn(vb, kd)
        st_ref[hh] = st_new
        new_states.append(st_new)

        o = o[:rows]
        ug = ug_ref[:, sl]
        o = o * lax.rsqrt(jnp.mean(o * o, axis=-1, keepdims=True) + RMS_EPS) * nw_ref[...] * (ug * _sigmoid(ug))
        o_ref[:, sl] = o.astype(o_ref.dtype)

    @pl.when(ci == nchunks - 1)
    def _():
        for hh in range(heads):
            sout_ref[hh] = new_states[hh].T


def hgrn2(u, lb, norm_w, s0, bsz, seq, out_dtype):
    assert A_KDIM == LANES and A_VDIM == LANES
    rows = min(seq, HGRN_CHUNK)
    assert seq % rows == 0
    chunk = max(rows, 2 * SUBLANES)
    nchunks = seq // rows
    lvl = jnp.asarray(_level_matrix(chunk))
    hg = min(HGRN_HEADS, A_HEADS)
    assert A_HEADS % hg == 0
    width = hg * LANES

    def u_spec(group):
        assert IN_OFFS[group] % width == 0
        off = IN_OFFS[group] // width
        return pl.BlockSpec((rows, width), lambda b, hh, c: (b * nchunks + c, off + hh))

    vec_spec = pl.BlockSpec((1, width), lambda b, hh, c: (0, hh))
    state_spec = pl.BlockSpec((None, hg, A_KDIM, A_VDIM), lambda b, hh, c: (b, hh, 0, 0))
    return pl.pallas_call(
        functools.partial(_hgrn_body, rows=rows, chunk=chunk, nchunks=nchunks, heads=hg),
        grid=(bsz, A_HEADS // hg, nchunks),
        in_specs=[u_spec(0), u_spec(1), u_spec(2), u_spec(3), vec_spec,
                  pl.BlockSpec((1, LANES), lambda b, hh, c: (0, 0)),
                  state_spec,
                  pl.BlockSpec((chunk, chunk), lambda b, hh, c: (0, 0))],
        out_specs=(pl.BlockSpec((rows, width), lambda b, hh, c: (b * nchunks + c, hh)), state_spec),
        out_shape=(jax.ShapeDtypeStruct((bsz * seq, A_VW), out_dtype),
                   jax.ShapeDtypeStruct((bsz, A_HEADS, A_KDIM, A_VDIM), F32)),
        scratch_shapes=[pltpu.VMEM((hg, A_VDIM, A_KDIM), F32)],
        compiler_params=_cparams("parallel", "parallel", "arbitrary"),
        name="hgrn2",
    )(u, u, u, u, lb.reshape(1, A_KW), norm_w.reshape(1, A_VDIM), s0, lvl)


def _moba_prompt_body(q_ref, k_ref, v_ref, cb_ref, o_ref, kt_ref, vb_ref, km_ref, s_ref, p_ref, *, seq, nblk):
    kf = k_ref[...]
    kt_ref[...] = kf.T.astype(BF16)
    vb_ref[...] = v_ref[...].astype(BF16)
    km_ref[...] = jnp.zeros_like(km_ref)
    km_ref[0:nblk, :] = jnp.sum(kf.reshape(nblk, MOBA_BLOCK, B_HDIM), axis=1) * (1.0 / MOBA_BLOCK)

    qb, kb = MOBA_QBLOCK, MOBA_BLOCK
    k_sel = min(MOBA_TOPK, nblk - 1)
    c2 = (B_HDIM ** -0.5) * math.log2(math.e)
    lane = lax.broadcasted_iota(jnp.int32, (qb, LANES), 1)

    for qi in range(seq // qb):
        q0 = qi * qb
        own = q0 // kb
        width = (own + 1) * kb
        qs = q_ref[q0:q0 + qb, :]
        qsb = qs.astype(BF16)
        bias = None
        if own > k_sel:
            gate = _dot_nt(qs, km_ref[...], precision=HIGHEST)
            gm = jnp.where(lane < own, gate, NEG_BIG)
            rank = jnp.zeros((qb, LANES), jnp.int32)
            for n in range(own):
                cm = gm[:, n:n + 1]
                ahead = (cm > gm) | ((cm == gm) & (lane > n))
                rank = rank + jnp.where(ahead, 1, 0)
            bias = jnp.where(rank < k_sel, 0.0, NEG_BIG)

        mx = None
        for n in range(own + 1):
            cols = slice(n * kb, (n + 1) * kb)
            sn = _dot(qsb, kt_ref[:, cols])
            if n == own:
                sn = sn + cb_ref[(q0 - own * kb) // qb]
            elif bias is not None:
                sn = sn + bias[:, n:n + 1]
            s_ref[qi % 2, :, cols] = sn
            part = sn[:, :LANES]
            for j in range(1, kb // LANES):
                part = jnp.maximum(part, sn[:, j * LANES:(j + 1) * LANES])
            mx = part if mx is None else jnp.maximum(mx, part)
        mc = jnp.max(mx, axis=-1, keepdims=True) * c2
        ls = None
        for n in range(own + 1):
            cols = slice(n * kb, (n + 1) * kb)
            p = jnp.exp2(s_ref[qi % 2, :, cols] * c2 - mc)
            p_ref[qi % 2, :, cols] = p.astype(BF16)
            for j in range(kb // LANES):
                pj = p[:, j * LANES:(j + 1) * LANES]
                ls = pj if ls is None else ls + pj
        l = jnp.sum(ls, axis=-1, keepdims=True)
        o = _dot(p_ref[qi % 2, :, 0:width], vb_ref[0:width, :]) / l
        o_ref[q0:q0 + qb, :] = o.astype(o_ref.dtype)


def moba_prompt(q, k, v, bsz, seq):
    assert seq % MOBA_BLOCK == 0 and B_HDIM == LANES and seq // MOBA_BLOCK <= LANES
    assert MOBA_BLOCK % MOBA_QBLOCK == 0
    nblk = seq // MOBA_BLOCK
    per = MOBA_BLOCK // MOBA_QBLOCK
    r = np.arange(MOBA_QBLOCK)[None, :, None] + MOBA_QBLOCK * np.arange(per)[:, None, None]
    causal = np.where(np.arange(MOBA_BLOCK)[None, None, :] <= r, 0.0, NEG_BIG).astype(np.float32)
    spec = pl.BlockSpec((seq, B_HDIM), lambda b, h: (b, h))
    return pl.pallas_call(
        functools.partial(_moba_prompt_body, seq=seq, nblk=nblk),
        grid=(bsz, B_HEADS),
        in_specs=[spec, spec, spec,
                  pl.BlockSpec((per, MOBA_QBLOCK, MOBA_BLOCK), lambda b, h: (0, 0, 0))],
        out_specs=spec,
        out_shape=jax.ShapeDtypeStruct((bsz * seq, B_W), BF16),
        scratch_shapes=[pltpu.VMEM((B_HDIM, seq), BF16), pltpu.VMEM((seq, B_HDIM), BF16),
                        pltpu.VMEM((LANES, B_HDIM), F32),
                        pltpu.VMEM((2, MOBA_QBLOCK, seq), F32), pltpu.VMEM((2, MOBA_QBLOCK, seq), BF16)],
        compiler_params=_cparams("parallel", "parallel"),
        name="moba_prompt",
    )(q, k, v, jnp.asarray(causal))


def _moba_select_body(pt_ref, *refs, tq, nblk, npages_step):
    page_refs = refs[:npages_step]
    q_ref, o_ref, km_ref = refs[npages_step:]
    j = pl.program_id(1)
    per_block = MOBA_BLOCK // PAGE_SIZE
    blocks_step = npages_step // per_block
    for i in range(blocks_step):
        acc = jnp.sum(page_refs[i * per_block][...], axis=0)
        for p in range(1, per_block):
            acc = acc + jnp.sum(page_refs[i * per_block + p][...], axis=0)
        row0 = pl.multiple_of((j * blocks_step + i) * B_HEADS, B_HEADS)
        km_ref[pl.ds(row0, B_HEADS), :] = acc * (1.0 / MOBA_BLOCK)

    @pl.when(j == pl.num_programs(1) - 1)
    def _():
        k_sel = min(MOBA_TOPK, nblk)
        lane_g = lax.broadcasted_iota(jnp.int32, (tq, nblk), 1)
        lane_o = lax.broadcasted_iota(jnp.int32, (tq, LANES), 1)
        for h in range(B_HEADS):
            kmh = km_ref[pl.ds(h, nblk, stride=B_HEADS), :]
            g = _dot_nt(q_ref[:, h * B_HDIM:(h + 1) * B_HDIM], kmh, precision=HIGHEST)
            res = jnp.zeros((tq, LANES), jnp.int32)
            for r in range(k_sel):
                mx = jnp.max(g, axis=-1, keepdims=True)
                ix = jnp.min(jnp.where(g == mx, lane_g, nblk), axis=-1, keepdims=True)
                res = jnp.where(lane_o == r, ix, res)
                g = jnp.where(lane_g == ix, -jnp.inf, g)
            o_ref[h * tq:(h + 1) * tq, :] = res


def moba_select(q, cache_k, page_table, tq):
    dbsz, npages = page_table.shape
    nblk = (PAST_LEN // MOBA_BLOCK)
    assert PAST_LEN % MOBA_BLOCK == 0 and npages * PAGE_SIZE == PAST_LEN
    ps = min(SEL_PAGES, npages)
    assert npages % ps == 0 and ps % (MOBA_BLOCK // PAGE_SIZE) == 0

    def page_spec(i):
        return pl.BlockSpec((None, PAGE_SIZE, B_HEADS, B_HDIM),
                            lambda b, j, pt: (pt[b, j * ps + i], 0, 0, 0))

    grid_spec = pltpu.PrefetchScalarGridSpec(
        num_scalar_prefetch=1,
        grid=(dbsz, npages // ps),
        in_specs=[page_spec(i) for i in range(ps)]
        + [pl.BlockSpec((tq, B_W), lambda b, j, pt: (b, 0))],
        out_specs=pl.BlockSpec((None, B_HEADS * tq, LANES), lambda b, j, pt: (b, 0, 0)),
        scratch_shapes=[pltpu.VMEM((nblk * B_HEADS, B_HDIM), F32)],
    )
    return pl.pallas_call(
        functools.partial(_moba_select_body, tq=tq, nblk=nblk, npages_step=ps),
        grid_spec=grid_spec,
        out_shape=jax.ShapeDtypeStruct((dbsz, B_HEADS * tq, LANES), jnp.int32),
        compiler_params=_cparams("parallel", "arbitrary"),
        name="moba_select",
    )(page_table, *([cache_k] * ps), q)


def _moba_decode_body(pt_ref, ix_ref, q_ref, kn_ref, vn_ref, ck_ref, cv_ref, o_ref,
                      kbuf, vbuf, sems, *, tq, k_sel, nsteps):
    b = pl.program_id(0)
    h = pl.program_id(1)
    step = b * B_HEADS + h
    slot = step % 2
    per_block = MOBA_BLOCK // PAGE_SIZE
    nsel = tq * k_sel * MOBA_BLOCK
    own_rows = kbuf.shape[1] - nsel

    def copies(st, sl):
        bb, hh = st // B_HEADS, st % B_HEADS
        out = []
        for t in range(tq):
            for s in range(k_sel):
                blk = ix_ref[(st * tq + t) * k_sel + s]
                for p in range(per_block):
                    page = pt_ref[bb, blk * per_block + p]
                    dst = pl.ds(((t * k_sel + s) * per_block + p) * PAGE_SIZE, PAGE_SIZE)
                    out.append(pltpu.make_async_copy(ck_ref.at[page, :, hh, :], kbuf.at[sl, dst, :], sems.at[0, sl]))
                    out.append(pltpu.make_async_copy(cv_ref.at[page, :, hh, :], vbuf.at[sl, dst, :], sems.at[1, sl]))
        return out

    @pl.when(step == 0)
    def _():
        for cp in copies(step, slot):
            cp.start()

    @pl.when(step + 1 < nsteps)
    def _():
        for cp in copies(step + 1, 1 - slot):
            cp.start()

    zpad = jnp.zeros((own_rows - tq, B_HDIM), F32)
    kbuf[slot, pl.ds(nsel, own_rows), :] = jnp.concatenate([kn_ref[...], zpad], axis=0)
    vbuf[slot, pl.ds(nsel, own_rows), :] = jnp.concatenate([vn_ref[...], zpad], axis=0)
    for cp in copies(step, slot):
        cp.wait()

    ncol = kbuf.shape[1]
    qb = q_ref[...].astype(BF16)
    s = _dot_nt(qb, kbuf[slot].astype(BF16)) * (B_HDIM ** -0.5)
    col = lax.broadcasted_iota(jnp.int32, (tq, ncol), 1)
    row = lax.broadcasted_iota(jnp.int32, (tq, ncol), 0)
    lo = row * (k_sel * MOBA_BLOCK)
    allowed = ((col >= lo) & (col < lo + k_sel * MOBA_BLOCK)) | ((col >= nsel) & (col <= nsel + row))
    s = jnp.where(allowed, s, NEG_BIG)
    m = jnp.max(s, axis=-1, keepdims=True)
    p = jnp.exp(s - m)
    l = jnp.sum(p, axis=-1, keepdims=True)
    o_ref[...] = _dot(p.astype(BF16), vbuf[slot].astype(BF16)) / l


def moba_decode(q, kn, vn, cache_k, cache_v, page_table, sel, tq):
    dbsz = page_table.shape[0]
    k_sel = min(MOBA_TOPK, PAST_LEN // MOBA_BLOCK)
    assert PAST_LEN % MOBA_BLOCK == 0, "cached rows of the own block are not supported"
    assert k_sel > 0 and tq <= LANES
    nrows = tq * k_sel * MOBA_BLOCK + LANES
    spec = pl.BlockSpec((tq, B_HDIM), lambda b, h, pt, ix: (b, h))
    grid_spec = pltpu.PrefetchScalarGridSpec(
        num_scalar_prefetch=2,
        grid=(dbsz, B_HEADS),
        in_specs=[spec, spec, spec,
                  pl.BlockSpec(memory_space=pl.ANY), pl.BlockSpec(memory_space=pl.ANY)],
        out_specs=spec,
        scratch_shapes=[pltpu.VMEM((2, nrows, B_HDIM), F32), pltpu.VMEM((2, nrows, B_HDIM), F32),
                        pltpu.SemaphoreType.DMA((2, 2))],
    )
    return pl.pallas_call(
        functools.partial(_moba_decode_body, tq=tq, k_sel=k_sel, nsteps=dbsz * B_HEADS),
        grid_spec=grid_spec,
        out_shape=jax.ShapeDtypeStruct((dbsz * tq, B_W), F32),
        compiler_params=_cparams("arbitrary", "arbitrary"),
        name="moba_decode",
    )(page_table, sel, q, kn, vn, cache_k, cache_v)


def _merge_body(oa_ref, ob_ref, ga_ref, gb_ref, wa_ref, wb_ref, o_ref):
    a = _dot(oa_ref[...].astype(BF16), wa_ref[...])
    b = _dot(ob_ref[...].astype(BF16), wb_ref[...])
    o_ref[...] = (_sigmoid(ga_ref[...]) * a + _sigmoid(gb_ref[...]) * b).astype(o_ref.dtype)


def merge_branches(o_a, o_b, u, w_a, w_b, bm):
    n = o_a.shape[0]
    bn = min(MM_COLS, D_MODEL)
    assert IN_OFFS[7] % bn == 0 and IN_OFFS[8] % bn == 0 and D_MODEL % bn == 0
    ca, cb = IN_OFFS[7] // bn, IN_OFFS[8] // bn
    return pl.pallas_call(
        _merge_body,
        grid=(D_MODEL // bn, n // bm),
        in_specs=[pl.BlockSpec((bm, A_VW), lambda j, i: (i, 0)),
                  pl.BlockSpec((bm, B_W), lambda j, i: (i, 0)),
                  pl.BlockSpec((bm, bn), lambda j, i: (i, ca + j)),
                  pl.BlockSpec((bm, bn), lambda j, i: (i, cb + j)),
                  pl.BlockSpec((A_VW, bn), lambda j, i: (0, j)),
                  pl.BlockSpec((B_W, bn), lambda j, i: (0, j))],
        out_specs=pl.BlockSpec((bm, bn), lambda j, i: (i, j)),
        out_shape=jax.ShapeDtypeStruct((n, D_MODEL), BF16),
        compiler_params=_cparams("parallel", "parallel"),
        name="merge_branches",
    )(o_a, o_b, u, u, w_a, w_b)


def _pack_words(h):
    half = h.shape[1] // 2
    lo = pltpu.bitcast(h[:, :half].astype(BF16).astype(F32), jnp.uint32)
    hi = pltpu.bitcast(h[:, half:].astype(BF16).astype(F32), jnp.uint32)
    return (lo >> 16) | (hi & jnp.uint32(0xFFFF0000))


def _router_body(xp_ref, xs_ref, nw_ref, wr_ref, br_ref, hp_ref, idx_ref, gate_ref, *, n_prompt_tiles):
    i = pl.program_id(0)
    x1 = jnp.where(i < n_prompt_tiles, xp_ref[...], xs_ref[...])
    h2 = _rms(x1, nw_ref[...])
    words = _pack_words(h2)
    for s in range(WORDS_PER_ROW):
        hp_ref[s] = words[:, s * LANES:(s + 1) * LANES]
    h_hi = h2.astype(BF16)
    h_lo = (h2 - h_hi.astype(F32)).astype(BF16)
    a = _dot(h_hi, wr_ref[...])
    b = _dot(h_lo, wr_ref[...])
    logits = a + pltpu.roll(a, LANES - N_EXPERTS, 1) + b + br_ref[...]
    lane = lax.broadcasted_iota(jnp.int32, logits.shape, 1)
    g = jnp.where(lane < N_EXPERTS, logits, -jnp.inf)
    idx = jnp.zeros(logits.shape, jnp.int32)
    val = jnp.zeros(logits.shape, F32)
    top = None
    for r in range(TOP_K):
        mx = jnp.max(g, axis=-1, keepdims=True)
        ix = jnp.min(jnp.where(g == mx, lane, LANES), axis=-1, keepdims=True)
        top = mx if top is None else top
        idx = jnp.where(lane == r, ix, idx)
        val = jnp.where(lane == r, jnp.exp(mx - top), val)
        g = jnp.where(lane == ix, -jnp.inf, g)
    idx_ref[...] = idx
    gate_ref[...] = val / jnp.sum(val, axis=-1, keepdims=True)


def _dual_maps(tp):
    return (lambda i: (jnp.minimum(i, tp - 1), 0)), (lambda i: (jnp.maximum(i - tp, 0), 0))


def moe_router(x1_p, x1_s, norm_w, w_router, b_router):
    bm = ROW_TILE
    n_p, n_s = x1_p.shape[0], x1_s.shape[0]
    assert n_p % bm == 0 and n_s % bm == 0 and 2 * N_EXPERTS <= LANES and TOP_K <= LANES
    tp, ts = n_p // bm, n_s // bm
    n = n_p + n_s
    w_hi = w_router.astype(BF16)
    w_lo = (w_router - w_hi.astype(F32)).astype(BF16)
    wr = jnp.zeros((D_MODEL, LANES), BF16).at[:, :N_EXPERTS].set(w_hi).at[:, N_EXPERTS:2 * N_EXPERTS].set(w_lo)
    br = jnp.zeros((1, LANES), F32).at[0, :N_EXPERTS].set(b_router)
    p_map, s_map = _dual_maps(tp)
    full = lambda i: (0, 0)
    row = lambda i: (i, 0)
    return pl.pallas_call(
        functools.partial(_router_body, n_prompt_tiles=tp),
        grid=(tp + ts,),
        in_specs=[pl.BlockSpec((bm, D_MODEL), p_map), pl.BlockSpec((bm, D_MODEL), s_map),
                  pl.BlockSpec((1, D_MODEL), full),
                  pl.BlockSpec((D_MODEL, LANES), full), pl.BlockSpec((1, LANES), full)],
        out_specs=(pl.BlockSpec((WORDS_PER_ROW, bm, LANES), lambda i: (0, i, 0)),
                   pl.BlockSpec((bm, LANES), row), pl.BlockSpec((bm, LANES), row)),
        out_shape=(jax.ShapeDtypeStruct((WORDS_PER_ROW, n, LANES), jnp.uint32),
                   jax.ShapeDtypeStruct((n, LANES), jnp.int32),
                   jax.ShapeDtypeStruct((n, LANES), F32)),
        compiler_params=_cparams("parallel"),
        name="moe_router",
    )(x1_p, x1_s, norm_w.reshape(1, D_MODEL), wr, br)


def _rank_body(idx_ref, rank_ref, cnt_ref, carry_ref):
    i = pl.program_id(0)

    @pl.when(i == 0)
    def _():
        carry_ref[...] = jnp.zeros_like(carry_ref)

    idx = idx_ref[...]
    bt = idx.shape[0]
    lane = lax.broadcasted_iota(jnp.int32, (bt, LANES), 1)
    rowi = lax.broadcasted_iota(jnp.int32, (bt, bt), 0)
    coli = lax.broadcasted_iota(jnp.int32, (bt, bt), 1)
    before = jnp.where(rowi > coli, 1.0, 0.0).astype(BF16)
    base = carry_ref[...]
    res = jnp.zeros((bt, LANES), jnp.int32)
    for k in range(TOP_K):
        onehot = jnp.where(lane == idx[:, k:k + 1], 1.0, 0.0)
        earlier = _dot(before, onehot.astype(BF16))
        rk = jnp.sum(onehot * (earlier + base), axis=-1, keepdims=True)
        res = jnp.where(lane == k, rk.astype(jnp.int32), res)
        base = base + jnp.sum(onehot, axis=0, keepdims=True)
    rank_ref[...] = res
    carry_ref[...] = base
    cnt_ref[...] = jnp.broadcast_to(base, cnt_ref.shape).astype(jnp.int32)


def moe_rank(idx):
    n = idx.shape[0]
    bt = ROW_TILE
    return pl.pallas_call(
        _rank_body,
        grid=(n // bt,),
        in_specs=[pl.BlockSpec((bt, LANES), lambda i: (i, 0))],
        out_specs=(pl.BlockSpec((bt, LANES), lambda i: (i, 0)),
                   pl.BlockSpec((SUBLANES, LANES), lambda i: (0, 0))),
        out_shape=(jax.ShapeDtypeStruct((n, LANES), jnp.int32),
                   jax.ShapeDtypeStruct((SUBLANES, LANES), jnp.int32)),
        scratch_shapes=[pltpu.VMEM((1, LANES), F32)],
        compiler_params=_cparams("arbitrary"),
        name="moe_rank",
    )(idx)


def _load_slots(dest_ref, dsm, sem_idx):
    cp = pltpu.make_async_copy(dest_ref.at[pl.program_id(0)], dsm, sem_idx)
    cp.start()
    cp.wait()


def _dispatch_body(pe_ref, nu_ref, hp_ref, dest_ref, xs_ref, dsm, zbuf, sem_idx, sem, zsem, *, bt, n_blocks):
    wpr = WORDS_PER_ROW

    @pl.when(pl.program_id(0) == 0)
    def _():
        zbuf[...] = jnp.zeros_like(zbuf)

        def clear(row0):
            return pltpu.make_async_copy(
                zbuf, xs_ref.at[pl.ds(pl.multiple_of(row0 * wpr, wpr), MOE_ROWS * wpr), :], zsem)

        def seg_tail(e):
            return jnp.maximum(pe_ref[e] - MOE_ROWS, 0)

        for e in range(N_EXPERTS):
            clear(seg_tail(e)).start()
        lax.fori_loop(nu_ref[0], n_blocks, lambda b, c: (clear(b * MOE_ROWS).start(), c)[1], 0)
        for e in range(N_EXPERTS):
            clear(seg_tail(e)).wait()
        lax.fori_loop(nu_ref[0], n_blocks, lambda b, c: (clear(b * MOE_ROWS).wait(), c)[1], 0)

    _load_slots(dest_ref, dsm, sem_idx)

    def issue(r, c):
        for k in range(TOP_K):
            d = dsm[r * TOP_K + k]
            pltpu.make_async_copy(hp_ref.at[:, r, :],
                                  xs_ref.at[pl.ds(pl.multiple_of(d * wpr, wpr), wpr), :],
                                  sem).start(priority=k % 2)
        return c

    lax.fori_loop(0, bt, issue, 0, unroll=DMA_UNROLL)
    for k in range(TOP_K):
        whole = xs_ref.at[pl.ds(0, bt * wpr), :]
        pltpu.make_async_copy(whole, whole, sem).wait()


def moe_dispatch(hp, dest, pends, n_used, n_blocks):
    bt = ROW_TILE
    nt = dest.shape[0]
    wpr = WORDS_PER_ROW
    grid_spec = pltpu.PrefetchScalarGridSpec(
        num_scalar_prefetch=2,
        grid=(nt,),
        in_specs=[pl.BlockSpec((wpr, bt, LANES), lambda i, pe, nu: (0, i, 0)),
                  pl.BlockSpec(memory_space=pl.ANY)],
        out_specs=pl.BlockSpec(memory_space=pl.ANY),
        scratch_shapes=[pltpu.SMEM((bt * TOP_K,), jnp.int32),
                        pltpu.VMEM((MOE_ROWS * wpr, LANES), jnp.uint32),
                        pltpu.SemaphoreType.DMA, pltpu.SemaphoreType.DMA, pltpu.SemaphoreType.DMA],
    )
    return pl.pallas_call(
        functools.partial(_dispatch_body, bt=bt, n_blocks=n_blocks),
        grid_spec=grid_spec,
        out_shape=jax.ShapeDtypeStruct((n_blocks * MOE_ROWS * wpr, LANES), jnp.uint32),
        compiler_params=_cparams("arbitrary"),
        name="moe_dispatch",
    )(pends, n_used, hp, dest)


def _experts_body(be_ref, nu_ref, x_ref, w1g_ref, w1l_ref, b1g_ref, b1l_ref, w2_ref, b2_ref,
                  o_ref, xb_ref, acc_ref, *, rows):
    i = pl.program_id(0)
    c = pl.program_id(1)
    wpr = WORDS_PER_ROW

    @pl.when(i < nu_ref[0])
    def _():
        @pl.when(c == 0)
        def _():
            half = wpr * LANES
            for s in range(wpr):
                w = x_ref[pl.ds(s, rows, stride=wpr), :]
                lo = pltpu.bitcast(w << 16, F32)
                hi = pltpu.bitcast(w & jnp.uint32(0xFFFF0000), F32)
                xb_ref[:, s * LANES:(s + 1) * LANES] = lo.astype(BF16)
                xb_ref[:, half + s * LANES:half + (s + 1) * LANES] = hi.astype(BF16)
            acc_ref[...] = jnp.broadcast_to(b2_ref[...], acc_ref.shape)

        xb = xb_ref[...]
        gu = _dot(xb, w1g_ref[...]) + b1g_ref[...]
        li = _dot(xb, w1l_ref[...]) + b1l_ref[...]
        glu = jnp.minimum(gu, SWIGLU_LIMIT)
        lin = jnp.clip(li, -SWIGLU_LIMIT, SWIGLU_LIMIT)
        act = glu * _sigmoid(SWIGLU_ALPHA * glu) * (lin + 1.0)
        acc_ref[...] += _dot(act.astype(BF16), w2_ref[...])

        @pl.when(c == pl.num_programs(1) - 1)
        def _():
            words = _pack_words(acc_ref[...])
            for s in range(wpr):
                o_ref[pl.ds(s, rows, stride=wpr), :] = words[:, s * LANES:(s + 1) * LANES]

    @pl.when((i >= nu_ref[0]) & (c == 0))
    def _():
        o_ref[...] = jnp.zeros_like(o_ref)


def moe_experts(xs, blk_exp, n_used, w1, b1, w2, b2):
    rows, ff = MOE_ROWS, min(MOE_FF, D_FF)
    wpr = WORDS_PER_ROW
    n_rows = xs.shape[0] // wpr
    nb = n_rows // rows
    nc = D_FF // ff
    lin_off = D_FF // ff

    def eff(i, c, be, nu):
        live = i < nu[0]
        return jnp.where(live, i, nu[0] - 1), jnp.where(live, c, nc - 1)

    def x_map(i, c, be, nu):
        return (eff(i, c, be, nu)[0], 0)

    def w1g_map(i, c, be, nu):
        ii, cc = eff(i, c, be, nu)
        return (be[ii], 0, cc)

    def w1l_map(i, c, be, nu):
        ii, cc = eff(i, c, be, nu)
        return (be[ii], 0, lin_off + cc)

    def w2_map(i, c, be, nu):
        ii, cc = eff(i, c, be, nu)
        return (be[ii], cc, 0)

    def b2_map(i, c, be, nu):
        return (be[eff(i, c, be, nu)[0]], 0, 0)

    grid_spec = pltpu.PrefetchScalarGridSpec(
        num_scalar_prefetch=2,
        grid=(nb, nc),
        in_specs=[pl.BlockSpec((rows * wpr, LANES), x_map),
                  pl.BlockSpec((None, D_MODEL, ff), w1g_map),
                  pl.BlockSpec((None, D_MODEL, ff), w1l_map),
                  pl.BlockSpec((None, 1, ff), w1g_map),
                  pl.BlockSpec((None, 1, ff), w1l_map),
                  pl.BlockSpec((None, ff, D_MODEL), w2_map),
                  pl.BlockSpec((None, 1, D_MODEL), b2_map)],
        out_specs=pl.BlockSpec((rows * wpr, LANES), lambda i, c, be, nu: (i, 0)),
        scratch_shapes=[pltpu.VMEM((rows, D_MODEL), BF16), pltpu.VMEM((rows, D_MODEL), F32)],
    )
    return pl.pallas_call(
        functools.partial(_experts_body, rows=rows),
        grid_spec=grid_spec,
        out_shape=jax.ShapeDtypeStruct((n_rows * wpr, LANES), jnp.uint32),
        compiler_params=_cparams("arbitrary", "arbitrary"),
        name="moe_experts",
    )(blk_exp, n_used, xs, w1, w1, b1, b1, w2, b2)


def _tail_body(xp_ref, xs_ref, gate_ref, pp_ref, ps_ref, nw_ref, wg_ref, wp_ref, nf_ref, dest_ref, y_ref,
               op_ref, os_ref, buf, dsm, sem_idx, sems, *, bt, n_prompt_tiles, n_tiles):
    i = pl.program_id(0)
    slot = i % 2
    wpr = WORDS_PER_ROW
    nslots = bt * TOP_K

    def fetch(tile, sl):
        cp = pltpu.make_async_copy(dest_ref.at[tile], dsm.at[pl.ds(sl * nslots, nslots)], sem_idx)
        cp.start()
        cp.wait()

        def issue(r, c):
            for k in range(TOP_K):
                d = dsm[sl * nslots + r * TOP_K + k]
                pltpu.make_async_copy(y_ref.at[pl.ds(pl.multiple_of(d * wpr, wpr), wpr), :],
                                      buf.at[sl, k, :, r, :], sems.at[sl]).start(priority=k % 2)
            return c

        lax.fori_loop(0, bt, issue, 0, unroll=DMA_UNROLL)

    @pl.when(i == 0)
    def _():
        fetch(0, 0)

    @pl.when(i + 1 < n_tiles)
    def _():
        fetch(i + 1, 1 - slot)

    for k in range(TOP_K):
        whole = y_ref.at[pl.ds(0, bt * wpr), :]
        pltpu.make_async_copy(whole, whole, sems.at[slot]).wait()

    first = i < n_prompt_tiles
    gates = gate_ref[...]
    gk = [jnp.broadcast_to(gates[:, k:k + 1], (bt, LANES)) for k in range(TOP_K)]
    lo_cols, hi_cols = [], []
    for s in range(wpr):
        lo = hi = None
        for k in range(TOP_K):
            w = buf[slot, k, s]
            wl = gk[k] * pltpu.bitcast(w << 16, F32)
            wh = gk[k] * pltpu.bitcast(w & jnp.uint32(0xFFFF0000), F32)
            lo = wl if lo is None else lo + wl
            hi = wh if hi is None else hi + wh
        lo_cols.append(lo)
        hi_cols.append(hi)
    x = jnp.where(first, xp_ref[...], xs_ref[...]) + jnp.concatenate(lo_cols + hi_cols, axis=1)
    p = jnp.where(first, pp_ref[...], ps_ref[...])
    gate = _sigmoid(_dot(_rms(x, nw_ref[...]).astype(BF16), wg_ref[...]))
    x = x + gate * _dot(p.astype(BF16), wp_ref[...])
    out = _rms(x, nf_ref[...])

    @pl.when(first)
    def _():
        op_ref[...] = out

    @pl.when(jnp.logical_not(first))
    def _():
        os_ref[...] = out


def moe_combine_final(x1_p, x1_s, gates, dest, y, p_p, p_s, norm_ple, w_gate, w_ple, norm_final):
    bt = ROW_TILE
    n_p, n_s = x1_p.shape[0], x1_s.shape[0]
    tp, ts = n_p // bt, n_s // bt
    p_map, s_map = _dual_maps(tp)
    full = lambda i: (0, 0)
    return pl.pallas_call(
        functools.partial(_tail_body, bt=bt, n_prompt_tiles=tp, n_tiles=tp + ts),
        grid=(tp + ts,),
        in_specs=[pl.BlockSpec((bt, D_MODEL), p_map), pl.BlockSpec((bt, D_MODEL), s_map),
                  pl.BlockSpec((bt, LANES), lambda i: (i, 0)),
                  pl.BlockSpec((bt, PLE_DIM), p_map), pl.BlockSpec((bt, PLE_DIM), s_map),
                  pl.BlockSpec((1, D_MODEL), full), pl.BlockSpec((D_MODEL, D_MODEL), full),
                  pl.BlockSpec((PLE_DIM, D_MODEL), full), pl.BlockSpec((1, D_MODEL), full),
                  pl.BlockSpec(memory_space=pl.ANY),
                  pl.BlockSpec(memory_space=pl.ANY)],
        out_specs=(pl.BlockSpec((bt, D_MODEL), p_map), pl.BlockSpec((bt, D_MODEL), s_map)),
        out_shape=(jax.ShapeDtypeStruct((n_p, D_MODEL), F32), jax.ShapeDtypeStruct((n_s, D_MODEL), F32)),
        scratch_shapes=[pltpu.VMEM((2, TOP_K, WORDS_PER_ROW, bt, LANES), jnp.uint32),
                        pltpu.SMEM((2 * bt * TOP_K,), jnp.int32),
                        pltpu.SemaphoreType.DMA, pltpu.SemaphoreType.DMA((2,))],
        compiler_params=_cparams("arbitrary"),
        name="moe_combine_final",
    )(x1_p, x1_s, gates, p_p, p_s, norm_ple.reshape(1, D_MODEL), w_gate, w_ple,
      norm_final.reshape(1, D_MODEL), dest, y)


def _moe_ffn_tail(x1_p, x1_s, norm_w, w_router, b_router, w1, b1, w2, b2, tail_args):
    hp, idx, gates = moe_router(x1_p, x1_s, norm_w, w_router, b_router)
    n = idx.shape[0]
    bt = ROW_TILE
    rank, counts = moe_rank(idx)
    counts = counts[0, :N_EXPERTS]
    pcounts = (counts + MOE_ROWS - 1) // MOE_ROWS * MOE_ROWS
    pends = jnp.cumsum(pcounts)
    pstarts = pends - pcounts
    e_tok = idx[:, :TOP_K]
    dest = (pstarts[e_tok] + rank[:, :TOP_K]).astype(jnp.int32).reshape(n // bt, bt * TOP_K)
    n_blocks = -(-(n * TOP_K + N_EXPERTS * (MOE_ROWS - 1)) // MOE_ROWS)
    blk_start = jnp.arange(n_blocks, dtype=jnp.int32) * MOE_ROWS
    blk_exp = jnp.minimum(jnp.sum((pends[None, :] <= blk_start[:, None]).astype(jnp.int32), axis=1),
                          N_EXPERTS - 1).astype(jnp.int32)
    n_used = (pends[-1] // MOE_ROWS).astype(jnp.int32).reshape(1)
    xs = moe_dispatch(hp, dest, pends.astype(jnp.int32), n_used, n_blocks)
    y = moe_experts(xs, blk_exp, n_used, w1, b1, w2, b2)
    return moe_combine_final(x1_p, x1_s, gates, dest, y, *tail_args)


def _row_tile(n, pref):
    return pref if n % pref == 0 else ROW_TILE


def kernel(x_prompt, x_sample, cache_k, cache_v, state_hgrn, page_table, p_prompt, p_sample, hgrn_lb, norm_mix, w_in, hgrn_norm, w_a, w_b, w_o, norm_moe, w_router, b_router, w_moe1, b_moe1, w_moe2, b_moe2, norm_ple, w_ple, w_ple_gate, norm_final):
    assert DEPTH == 1
    bsz, seq, _ = x_prompt.shape
    dbsz, tq, _ = x_sample.shape
    n_p, n_s = bsz * seq, dbsz * tq
    l = 0
    lb = jnp.cumsum(jax.nn.softmax(hgrn_lb.astype(F32), axis=0), axis=0)[l]
    w_in_b = w_in[l].astype(BF16)
    w_a_b, w_b_b, w_o_b = w_a[l].astype(BF16), w_b[l].astype(BF16), w_o[l].astype(BF16)
    w1_b, w2_b = w_moe1[l].astype(BF16), w_moe2[l].astype(BF16)
    b1 = b_moe1[l].reshape(N_EXPERTS, 1, 2 * D_FF)
    b2 = b_moe2[l].reshape(N_EXPERTS, 1, D_MODEL)
    wg_b, wp_b = w_ple_gate[l].astype(BF16), w_ple[l].astype(BF16)
    bn = MM_COLS if IN_DIM % MM_COLS == 0 and D_MODEL % MM_COLS == 0 else B_W

    xp = x_prompt.reshape(n_p, D_MODEL)
    xs = x_sample.reshape(n_s, D_MODEL)

    def mixer_inputs(x, n):
        h = rmsnorm_cast(x, norm_mix[l], _row_tile(n, 512))
        return matmul(h, w_in_b, _row_tile(n, MM_ROWS), bn)

    u_p = mixer_inputs(xp, n_p)
    oa_p, st_p = hgrn2(u_p, lb, hgrn_norm[l], jnp.zeros((bsz, A_HEADS, A_KDIM, A_VDIM), F32), bsz, seq, BF16)
    rt = _row_tile(seq, 512)
    q_p, k_p, v_p = rope_qkv(u_p, _rope_tables(jnp.arange(seq)), rt if seq % rt == 0 else seq)
    ob_p = moba_prompt(q_p, k_p, v_p, bsz, seq)
    m_p = merge_branches(oa_p, ob_p, u_p, w_a_b, w_b_b, _row_tile(n_p, MM_ROWS))
    x1_p = matmul(m_p, w_o_b, _row_tile(n_p, MM_ROWS), bn, residual=xp)

    u_s = mixer_inputs(xs, n_s)
    oa_s, st_s = hgrn2(u_s, lb, hgrn_norm[l], state_hgrn[l], dbsz, tq, F32)
    tabs_s = tuple(jnp.tile(t, (dbsz, 1)) for t in _rope_tables(PAST_LEN + jnp.arange(tq)))
    q_s, k_s, v_s = rope_qkv(u_s, tabs_s, n_s)
    k_sel = min(MOBA_TOPK, PAST_LEN // MOBA_BLOCK)
    sel = moba_select(q_s, cache_k[l], page_table, tq)[:, :, :k_sel].reshape(-1)
    ob_s = moba_decode(q_s, k_s, v_s, cache_k[l], cache_v[l], page_table, sel, tq)
    m_s = merge_branches(oa_s, ob_s, u_s, w_a_b, w_b_b, _row_tile(n_s, MM_ROWS))
    x1_s = matmul(m_s, w_o_b, _row_tile(n_s, MM_ROWS), bn, residual=xs)

    tail_args = (p_prompt[l].reshape(n_p, PLE_DIM), p_sample[l].reshape(n_s, PLE_DIM),
                 norm_ple[l], wg_b, wp_b, norm_final)
    y_p, y_s = _moe_ffn_tail(x1_p, x1_s, norm_moe[l], w_router[l], b_router[l], w1_b, b1, w2_b, b2, tail_args)

    hs_p = (1, bsz, seq, B_HEADS, B_HDIM)
    hs_s = (1, dbsz, tq, B_HEADS, B_HDIM)
    return (y_p.reshape(bsz, seq, D_MODEL), y_s.reshape(dbsz, tq, D_MODEL),
            st_p[None], k_p.reshape(hs_p), v_p.reshape(hs_p),
            st_s[None], k_s.reshape(hs_s), v_s.reshape(hs_s))
```

```python
import functools
import math

import numpy as np
import jax
import jax.numpy as jnp
from jax import lax
from jax.experimental import pallas as pl
from jax.experimental.pallas import tpu as pltpu

D_MODEL = 2048
DEPTH = 1
PAST_LEN = 16384
PAGE_SIZE = 128
A_HEADS = 8
A_KDIM = 128
A_VDIM = 128
B_HEADS = 8
B_HDIM = 128
MOBA_BLOCK = 256
MOBA_TOPK = 3
MOBA_QBLOCK = 128
ROPE_THETA = 500000.0
ROT_DIM = B_HDIM // 4
N_EXPERTS = 32
TOP_K = 4
D_FF = D_MODEL
SWIGLU_LIMIT = 7.0
SWIGLU_ALPHA = 1.702
PLE_DIM = 256
RMS_EPS = 1e-6
NEG_BIG = -1e30

A_KW = A_HEADS * A_KDIM
A_VW = A_HEADS * A_VDIM
B_W = B_HEADS * B_HDIM
IN_WIDTHS = (A_KW, A_KW, A_VW, A_VW, B_W, B_W, B_W, D_MODEL, D_MODEL)
IN_DIM = sum(IN_WIDTHS)
IN_OFFS = tuple(int(v) for v in np.cumsum((0,) + IN_WIDTHS[:-1]))

LANES = 128
SUBLANES = 8
VMEM_LIMIT = 56 * 1024 * 1024

ROW_TILE = 256
MM_ROWS = 1024
MM_COLS = 1024
HGRN_CHUNK = 128
HGRN_HEADS = 8
MOE_ROWS = 512
MOE_FF = 1024
SEL_PAGES = 16
DMA_UNROLL = 8

F32 = jnp.float32
BF16 = jnp.bfloat16
HIGHEST = lax.Precision.HIGHEST
WORDS_PER_ROW = D_MODEL // LANES


def _cparams(*sem):
    return pltpu.CompilerParams(dimension_semantics=sem, vmem_limit_bytes=VMEM_LIMIT)


def _dot(a, b, **kw):
    return jnp.dot(a, b, preferred_element_type=F32, **kw)


def _dot_nt(a, b, **kw):
    return lax.dot_general(a, b, (((1,), (1,)), ((), ())), preferred_element_type=F32, **kw)


def _dot_tn(a, b, **kw):
    return lax.dot_general(a, b, (((0,), (0,)), ((), ())), preferred_element_type=F32, **kw)


def _sigmoid(x):
    return 1.0 / (1.0 + jnp.exp(-x))


def _rms(x, w):
    ms = jnp.mean(x * x, axis=-1, keepdims=True)
    return x * lax.rsqrt(ms + RMS_EPS) * w


def _rmsnorm_body(x_ref, w_ref, o_ref):
    o_ref[...] = _rms(x_ref[...], w_ref[...]).astype(o_ref.dtype)


def rmsnorm_cast(x, w, bm):
    n, d = x.shape
    return pl.pallas_call(
        _rmsnorm_body,
        grid=(n // bm,),
        in_specs=[pl.BlockSpec((bm, d), lambda i: (i, 0)),
                  pl.BlockSpec((1, d), lambda i: (0, 0))],
        out_specs=pl.BlockSpec((bm, d), lambda i: (i, 0)),
        out_shape=jax.ShapeDtypeStruct((n, d), BF16),
        compiler_params=_cparams("parallel"),
        name="rmsnorm_cast",
    )(x, w.reshape(1, d))


def _matmul_body(a_ref, w_ref, o_ref):
    o_ref[...] = _dot(a_ref[...], w_ref[...]).astype(o_ref.dtype)


def _matmul_res_body(a_ref, w_ref, x_ref, o_ref):
    o_ref[...] = x_ref[...] + _dot(a_ref[...], w_ref[...])


def matmul(a, w, bm, bn, residual=None):
    n, k = a.shape
    m = w.shape[1]
    in_specs = [pl.BlockSpec((bm, k), lambda j, i: (i, 0)),
                pl.BlockSpec((k, bn), lambda j, i: (0, j))]
    args = (a, w)
    if residual is not None:
        in_specs.append(pl.BlockSpec((bm, bn), lambda j, i: (i, j)))
        args = (a, w, residual)
    return pl.pallas_call(
        _matmul_body if residual is None else _matmul_res_body,
        grid=(m // bn, n // bm),
        in_specs=in_specs,
        out_specs=pl.BlockSpec((bm, bn), lambda j, i: (i, j)),
        out_shape=jax.ShapeDtypeStruct((n, m), F32),
        compiler_params=_cparams("parallel", "parallel"),
        name="in_proj" if residual is None else "out_proj",
    )(*args)


def _rope_tables(pos):
    half = ROT_DIM // 2
    inv_freq = jnp.power(ROPE_THETA, -jnp.arange(half, dtype=F32) * (2.0 / ROT_DIM))
    ang = pos.astype(F32)[:, None] * inv_freq[None, :]
    cos, sin = jnp.cos(ang), jnp.sin(ang)
    n = pos.shape[0]
    ones = jnp.ones((n, B_HDIM - ROT_DIM), F32)
    zeros = jnp.zeros((n, B_HDIM - ROT_DIM), F32)
    zh = jnp.zeros((n, half), F32)
    c = jnp.concatenate([cos, cos, ones], axis=1)
    s_up = jnp.concatenate([-sin, zh, zeros], axis=1)
    s_dn = jnp.concatenate([zh, sin, zeros], axis=1)
    return c, s_up, s_dn


def _rope_body(q_ref, k_ref, v_ref, c_ref, su_ref, sd_ref, qo_ref, ko_ref, vo_ref):
    c, su, sd = c_ref[...], su_ref[...], sd_ref[...]
    half = ROT_DIM // 2
    for h in range(B_HEADS):
        sl = slice(h * B_HDIM, (h + 1) * B_HDIM)
        for src, dst in ((q_ref, qo_ref), (k_ref, ko_ref)):
            x = src[:, sl]
            y = x * c + pltpu.roll(x, B_HDIM - half, 1) * su + pltpu.roll(x, half, 1) * sd
            dst[:, sl] = y
    vo_ref[...] = v_ref[...]


def rope_qkv(u, tabs, bm):
    n = u.shape[0]
    c, su, sd = tabs
    tb = c.shape[0] // bm
    cq, ck, cv = (IN_OFFS[4] // B_W, IN_OFFS[5] // B_W, IN_OFFS[6] // B_W)
    tab_spec = pl.BlockSpec((bm, B_HDIM), lambda i: (i % tb, 0))
    out_spec = pl.BlockSpec((bm, B_W), lambda i: (i, 0))
    out_sds = jax.ShapeDtypeStruct((n, B_W), F32)
    return pl.pallas_call(
        _rope_body,
        grid=(n // bm,),
        in_specs=[pl.BlockSpec((bm, B_W), lambda i: (i, cq)),
                  pl.BlockSpec((bm, B_W), lambda i: (i, ck)),
                  pl.BlockSpec((bm, B_W), lambda i: (i, cv)),
                  tab_spec, tab_spec, tab_spec],
        out_specs=(out_spec, out_spec, out_spec),
        out_shape=(out_sds, out_sds, out_sds),
        compiler_params=_cparams("parallel"),
        name="rope_qkv",
    )(u, u, u, c, su, sd)


def _level_matrix(c):
    t = np.arange(c)[:, None]
    s = np.arange(c)[None, :]
    x = np.bitwise_xor(t, s)
    lvl = np.where(x > 0, np.floor(np.log2(np.maximum(x, 1))).astype(np.int32) + 1, 0)
    return np.where(s > t, -1, lvl).astype(np.int32)


def _hgrn_body(*refs, rows, chunk, nchunks, heads, with_cast):
    if with_cast:
        (uq_ref, uf_ref, ui_ref, ug_ref, lb_ref, nw_ref, s0_ref, lvl_ref, wsrc_ref,
         o_ref, sout_ref, wdst_ref, st_ref) = refs
        wdst_ref[...] = wsrc_ref[...].astype(BF16)
    else:
        uq_ref, uf_ref, ui_ref, ug_ref, lb_ref, nw_ref, s0_ref, lvl_ref, o_ref, sout_ref, st_ref = refs
    ci = pl.program_id(2)

    def padded(x, fill):
        if rows == chunk:
            return x
        return jnp.concatenate([x, jnp.full((chunk - rows, x.shape[1]), fill, x.dtype)], axis=0)

    rowi = lax.broadcasted_iota(jnp.int32, (chunk, chunk), 0)
    coli = lax.broadcasted_iota(jnp.int32, (chunk, chunk), 1)
    tri = jnp.where(rowi >= coli, 1.0, 0.0).astype(BF16)
    rsub = lax.broadcasted_iota(jnp.int32, (chunk, A_KDIM), 0)
    lvl = lvl_ref[...]

    @pl.when(ci == 0)
    def _():
        for hh in range(heads):
            st_ref[hh] = s0_ref[hh].T

    acts, terms = [], []
    for hh in range(heads):
        sl = slice(hh * LANES, (hh + 1) * LANES)
        lb = lb_ref[:, sl]
        uq = uq_ref[:, sl]
        q = padded(uq * _sigmoid(uq) * (A_KDIM ** -0.5), 0.0)
        f = padded(lb + (1.0 - lb) * _sigmoid(uf_ref[:, sl]), 1.0)
        g = jnp.log(f)
        acts.append((q, 1.0 - f, padded(ui_ref[:, sl], 0.0).astype(BF16)))
        hi = g.astype(BF16)
        r1 = g - hi.astype(F32)
        mid = r1.astype(BF16)
        terms += [hi, mid, (r1 - mid.astype(F32)).astype(BF16)]
    cums = _dot(tri, jnp.concatenate(terms, axis=1))

    new_states = []
    for hh in range(heads):
        sl = slice(hh * LANES, (hh + 1) * LANES)
        q, kk, vb = acts[hh]
        c0 = 3 * hh * LANES
        b = cums[:, c0:c0 + LANES] + cums[:, c0 + LANES:c0 + 2 * LANES] + cums[:, c0 + 2 * LANES:c0 + 3 * LANES]

        scores = jnp.where(lvl == 0, _dot_nt(q.astype(BF16), kk.astype(BF16)), 0.0)
        x = b
        blk, level = 1, 1
        while blk < chunk:
            nxt = pltpu.roll(x, chunk - blk, 0)
            qt = (q * jnp.exp(jnp.minimum(b - x, 0.0))).astype(BF16)
            kt = (kk * jnp.exp(jnp.minimum(nxt - b, 0.0))).astype(BF16)
            scores = jnp.where(lvl == level, _dot_nt(qt, kt), scores)
            x = jnp.where((rsub & blk) != 0, pltpu.roll(x, blk, 0), x)
            blk *= 2
            level += 1

        st = st_ref[hh]
        o = _dot(scores.astype(BF16), vb) + _dot_nt((q * jnp.exp(b)).astype(BF16), st.astype(BF16))
        b_end = b[chunk - 1:chunk, :]
        kd = (kk * jnp.exp(b_end - b)).astype(BF16)
        st_new = st * jnp.exp(b_end) + _dot_tn(vb, kd)
        st_ref[hh] = st_new
        new_states.append(st_new)

        o = o[:rows]
        ug = ug_ref[:, sl]
        o = o * lax.rsqrt(jnp.mean(o * o, axis=-1, keepdims=True) + RMS_EPS) * nw_ref[...] * (ug * _sigmoid(ug))
        o_ref[:, sl] = o.astype(o_ref.dtype)

    @pl.when(ci == nchunks - 1)
    def _():
        for hh in range(heads):
            sout_ref[hh] = new_states[hh].T


def hgrn2(u, lb, norm_w, s0, bsz, seq, out_dtype, cast_src=None):
    assert A_KDIM == LANES and A_VDIM == LANES
    rows = min(seq, HGRN_CHUNK)
    assert seq % rows == 0
    chunk = max(rows, 2 * SUBLANES)
    nchunks = seq // rows
    lvl = jnp.asarray(_level_matrix(chunk))
    hg = min(HGRN_HEADS, A_HEADS)
    assert A_HEADS % hg == 0
    width = hg * LANES

    def u_spec(group):
        assert IN_OFFS[group] % width == 0
        off = IN_OFFS[group] // width
        return pl.BlockSpec((rows, width), lambda b, hh, c: (b * nchunks + c, off + hh))

    vec_spec = pl.BlockSpec((1, width), lambda b, hh, c: (0, hh))
    state_spec = pl.BlockSpec((None, hg, A_KDIM, A_VDIM), lambda b, hh, c: (b, hh, 0, 0))
    ngroups = A_HEADS // hg
    in_specs = [u_spec(0), u_spec(1), u_spec(2), u_spec(3), vec_spec,
                pl.BlockSpec((1, LANES), lambda b, hh, c: (0, 0)),
                state_spec,
                pl.BlockSpec((chunk, chunk), lambda b, hh, c: (0, 0))]
    out_specs = [pl.BlockSpec((rows, width), lambda b, hh, c: (b * nchunks + c, hh)), state_spec]
    out_shape = [jax.ShapeDtypeStruct((bsz * seq, A_VW), out_dtype),
                 jax.ShapeDtypeStruct((bsz, A_HEADS, A_KDIM, A_VDIM), F32)]
    args = [u, u, u, u, lb.reshape(1, A_KW), norm_w.reshape(1, A_VDIM), s0, lvl]
    if cast_src is not None:
        assert cast_src.shape[0] == bsz * ngroups * nchunks
        slice_spec = pl.BlockSpec((None,) + cast_src.shape[1:],
                                  lambda b, hh, c: ((b * ngroups + hh) * nchunks + c, 0, 0))
        in_specs.append(slice_spec)
        out_specs.append(slice_spec)
        out_shape.append(jax.ShapeDtypeStruct(cast_src.shape, BF16))
        args.append(cast_src)
    return pl.pallas_call(
        functools.partial(_hgrn_body, rows=rows, chunk=chunk, nchunks=nchunks, heads=hg,
                          with_cast=cast_src is not None),
        grid=(bsz, ngroups, nchunks),
        in_specs=in_specs,
        out_specs=tuple(out_specs),
        out_shape=tuple(out_shape),
        scratch_shapes=[pltpu.VMEM((hg, A_VDIM, A_KDIM), F32)],
        compiler_params=_cparams("parallel", "parallel", "arbitrary"),
        name="hgrn2",
    )(*args)


def _moba_prompt_body(q_ref, k_ref, v_ref, cb_ref, o_ref, kt_ref, vb_ref, km_ref, s_ref, p_ref, *, seq, nblk):
    kf = k_ref[...]
    kt_ref[...] = kf.T.astype(BF16)
    vb_ref[...] = v_ref[...].astype(BF16)
    km_ref[...] = jnp.zeros_like(km_ref)
    km_ref[0:nblk, :] = jnp.sum(kf.reshape(nblk, MOBA_BLOCK, B_HDIM), axis=1) * (1.0 / MOBA_BLOCK)

    qb, kb = MOBA_QBLOCK, MOBA_BLOCK
    k_sel = min(MOBA_TOPK, nblk - 1)
    c2 = (B_HDIM ** -0.5) * math.log2(math.e)
    lane = lax.broadcasted_iota(jnp.int32, (qb, LANES), 1)

    for qi in range(seq // qb):
        q0 = qi * qb
        own = q0 // kb
        width = (own + 1) * kb
        qs = q_ref[q0:q0 + qb, :]
        qsb = qs.astype(BF16)
        bias = None
        if own > k_sel:
            gate = _dot_nt(qs, km_ref[...], precision=HIGHEST)
            gm = jnp.where(lane < own, gate, NEG_BIG)
            rank = jnp.zeros((qb, LANES), jnp.int32)
            for n in range(own):
                cm = gm[:, n:n + 1]
                ahead = (cm > gm) | ((cm == gm) & (lane > n))
                rank = rank + jnp.where(ahead, 1, 0)
            bias = jnp.where(rank < k_sel, 0.0, NEG_BIG)

        mx = None
        for n in range(own + 1):
            cols = slice(n * kb, (n + 1) * kb)
            sn = _dot(qsb, kt_ref[:, cols])
            if n == own:
                sn = sn + cb_ref[(q0 - own * kb) // qb]
            elif bias is not None:
                sn = sn + bias[:, n:n + 1]
            s_ref[qi % 2, :, cols] = sn
            part = sn[:, :LANES]
            for j in range(1, kb // LANES):
                part = jnp.maximum(part, sn[:, j * LANES:(j + 1) * LANES])
            mx = part if mx is None else jnp.maximum(mx, part)
        mc = jnp.max(mx, axis=-1, keepdims=True) * c2
        ls = None
        for n in range(own + 1):
            cols = slice(n * kb, (n + 1) * kb)
            p = jnp.exp2(s_ref[qi % 2, :, cols] * c2 - mc)
            p_ref[qi % 2, :, cols] = p.astype(BF16)
            for j in range(kb // LANES):
                pj = p[:, j * LANES:(j + 1) * LANES]
                ls = pj if ls is None else ls + pj
        l = jnp.sum(ls, axis=-1, keepdims=True)
        o = _dot(p_ref[qi % 2, :, 0:width], vb_ref[0:width, :]) / l
        o_ref[q0:q0 + qb, :] = o.astype(o_ref.dtype)


def moba_prompt(q, k, v, bsz, seq):
    assert seq % MOBA_BLOCK == 0 and B_HDIM == LANES and seq // MOBA_BLOCK <= LANES
    assert MOBA_BLOCK % MOBA_QBLOCK == 0
    nblk = seq // MOBA_BLOCK
    per = MOBA_BLOCK // MOBA_QBLOCK
    r = np.arange(MOBA_QBLOCK)[None, :, None] + MOBA_QBLOCK * np.arange(per)[:, None, None]
    causal = np.where(np.arange(MOBA_BLOCK)[None, None, :] <= r, 0.0, NEG_BIG).astype(np.float32)
    spec = pl.BlockSpec((seq, B_HDIM), lambda b, h: (b, h))
    return pl.pallas_call(
        functools.partial(_moba_prompt_body, seq=seq, nblk=nblk),
        grid=(bsz, B_HEADS),
        in_specs=[spec, spec, spec,
                  pl.BlockSpec((per, MOBA_QBLOCK, MOBA_BLOCK), lambda b, h: (0, 0, 0))],
        out_specs=spec,
        out_shape=jax.ShapeDtypeStruct((bsz * seq, B_W), BF16),
        scratch_shapes=[pltpu.VMEM((B_HDIM, seq), BF16), pltpu.VMEM((seq, B_HDIM), BF16),
                        pltpu.VMEM((LANES, B_HDIM), F32),
                        pltpu.VMEM((2, MOBA_QBLOCK, seq), F32), pltpu.VMEM((2, MOBA_QBLOCK, seq), BF16)],
        compiler_params=_cparams("parallel", "parallel"),
        name="moba_prompt",
    )(q, k, v, jnp.asarray(causal))


def _moba_select_body(pt_ref, *refs, tq, nblk, npages_step):
    page_refs = refs[:npages_step]
    q_ref, o_ref, km_ref = refs[npages_step:]
    j = pl.program_id(1)
    per_block = MOBA_BLOCK // PAGE_SIZE
    blocks_step = npages_step // per_block
    for i in range(blocks_step):
        acc = jnp.sum(page_refs[i * per_block][...], axis=0)
        for p in range(1, per_block):
            acc = acc + jnp.sum(page_refs[i * per_block + p][...], axis=0)
        row0 = pl.multiple_of((j * blocks_step + i) * B_HEADS, B_HEADS)
        km_ref[pl.ds(row0, B_HEADS), :] = acc * (1.0 / MOBA_BLOCK)

    @pl.when(j == pl.num_programs(1) - 1)
    def _():
        k_sel = min(MOBA_TOPK, nblk)
        lane_g = lax.broadcasted_iota(jnp.int32, (tq, nblk), 1)
        lane_o = lax.broadcasted_iota(jnp.int32, (tq, LANES), 1)
        for h in range(B_HEADS):
            kmh = km_ref[pl.ds(h, nblk, stride=B_HEADS), :]
            g = _dot_nt(q_ref[:, h * B_HDIM:(h + 1) * B_HDIM], kmh, precision=HIGHEST)
            res = jnp.zeros((tq, LANES), jnp.int32)
            for r in range(k_sel):
                mx = jnp.max(g, axis=-1, keepdims=True)
                ix = jnp.min(jnp.where(g == mx, lane_g, nblk), axis=-1, keepdims=True)
                res = jnp.where(lane_o == r, ix, res)
                g = jnp.where(lane_g == ix, -jnp.inf, g)
            o_ref[h * tq:(h + 1) * tq, :] = res


def moba_select(q, cache_k, page_table, tq):
    dbsz, npages = page_table.shape
    nblk = (PAST_LEN // MOBA_BLOCK)
    assert PAST_LEN % MOBA_BLOCK == 0 and npages * PAGE_SIZE == PAST_LEN
    ps = min(SEL_PAGES, npages)
    assert npages % ps == 0 and ps % (MOBA_BLOCK // PAGE_SIZE) == 0

    def page_spec(i):
        return pl.BlockSpec((None, PAGE_SIZE, B_HEADS, B_HDIM),
                            lambda b, j, pt: (pt[b, j * ps + i], 0, 0, 0))

    grid_spec = pltpu.PrefetchScalarGridSpec(
        num_scalar_prefetch=1,
        grid=(dbsz, npages // ps),
        in_specs=[page_spec(i) for i in range(ps)]
        + [pl.BlockSpec((tq, B_W), lambda b, j, pt: (b, 0))],
        out_specs=pl.BlockSpec((None, B_HEADS * tq, LANES), lambda b, j, pt: (b, 0, 0)),
        scratch_shapes=[pltpu.VMEM((nblk * B_HEADS, B_HDIM), F32)],
    )
    return pl.pallas_call(
        functools.partial(_moba_select_body, tq=tq, nblk=nblk, npages_step=ps),
        grid_spec=grid_spec,
        out_shape=jax.ShapeDtypeStruct((dbsz, B_HEADS * tq, LANES), jnp.int32),
        compiler_params=_cparams("parallel", "arbitrary"),
        name="moba_select",
    )(page_table, *([cache_k] * ps), q)


def _moba_decode_body(pt_ref, ix_ref, q_ref, kn_ref, vn_ref, ck_ref, cv_ref, o_ref,
                      kbuf, vbuf, sems, *, tq, k_sel, nsteps):
    b = pl.program_id(0)
    h = pl.program_id(1)
    step = b * B_HEADS + h
    slot = step % 2
    per_block = MOBA_BLOCK // PAGE_SIZE
    nsel = tq * k_sel * MOBA_BLOCK
    own_rows = kbuf.shape[1] - nsel

    def copies(st, sl):
        bb, hh = st // B_HEADS, st % B_HEADS
        out = []
        for t in range(tq):
            for s in range(k_sel):
                blk = ix_ref[(st * tq + t) * k_sel + s]
                for p in range(per_block):
                    page = pt_ref[bb, blk * per_block + p]
                    dst = pl.ds(((t * k_sel + s) * per_block + p) * PAGE_SIZE, PAGE_SIZE)
                    out.append(pltpu.make_async_copy(ck_ref.at[page, :, hh, :], kbuf.at[sl, dst, :], sems.at[0, sl]))
                    out.append(pltpu.make_async_copy(cv_ref.at[page, :, hh, :], vbuf.at[sl, dst, :], sems.at[1, sl]))
        return out

    @pl.when(step == 0)
    def _():
        for cp in copies(step, slot):
            cp.start()

    @pl.when(step + 1 < nsteps)
    def _():
        for cp in copies(step + 1, 1 - slot):
            cp.start()

    zpad = jnp.zeros((own_rows - tq, B_HDIM), F32)
    kbuf[slot, pl.ds(nsel, own_rows), :] = jnp.concatenate([kn_ref[...], zpad], axis=0)
    vbuf[slot, pl.ds(nsel, own_rows), :] = jnp.concatenate([vn_ref[...], zpad], axis=0)
    for cp in copies(step, slot):
        cp.wait()

    ncol = kbuf.shape[1]
    qb = q_ref[...].astype(BF16)
    s = _dot_nt(qb, kbuf[slot].astype(BF16)) * (B_HDIM ** -0.5)
    col = lax.broadcasted_iota(jnp.int32, (tq, ncol), 1)
    row = lax.broadcasted_iota(jnp.int32, (tq, ncol), 0)
    lo = row * (k_sel * MOBA_BLOCK)
    allowed = ((col >= lo) & (col < lo + k_sel * MOBA_BLOCK)) | ((col >= nsel) & (col <= nsel + row))
    s = jnp.where(allowed, s, NEG_BIG)
    m = jnp.max(s, axis=-1, keepdims=True)
    p = jnp.exp(s - m)
    l = jnp.sum(p, axis=-1, keepdims=True)
    o_ref[...] = _dot(p.astype(BF16), vbuf[slot].astype(BF16)) / l


def moba_decode(q, kn, vn, cache_k, cache_v, page_table, sel, tq):
    dbsz = page_table.shape[0]
    k_sel = min(MOBA_TOPK, PAST_LEN // MOBA_BLOCK)
    assert PAST_LEN % MOBA_BLOCK == 0, "cached rows of the own block are not supported"
    assert k_sel > 0 and tq <= LANES
    nrows = tq * k_sel * MOBA_BLOCK + LANES
    spec = pl.BlockSpec((tq, B_HDIM), lambda b, h, pt, ix: (b, h))
    grid_spec = pltpu.PrefetchScalarGridSpec(
        num_scalar_prefetch=2,
        grid=(dbsz, B_HEADS),
        in_specs=[spec, spec, spec,
                  pl.BlockSpec(memory_space=pl.ANY), pl.BlockSpec(memory_space=pl.ANY)],
        out_specs=spec,
        scratch_shapes=[pltpu.VMEM((2, nrows, B_HDIM), F32), pltpu.VMEM((2, nrows, B_HDIM), F32),
                        pltpu.SemaphoreType.DMA((2, 2))],
    )
    return pl.pallas_call(
        functools.partial(_moba_decode_body, tq=tq, k_sel=k_sel, nsteps=dbsz * B_HEADS),
        grid_spec=grid_spec,
        out_shape=jax.ShapeDtypeStruct((dbsz * tq, B_W), F32),
        compiler_params=_cparams("arbitrary", "arbitrary"),
        name="moba_decode",
    )(page_table, sel, q, kn, vn, cache_k, cache_v)


def _merge_body(oa_ref, ob_ref, ga_ref, gb_ref, wa_ref, wb_ref, o_ref):
    a = _dot(oa_ref[...].astype(BF16), wa_ref[...])
    b = _dot(ob_ref[...].astype(BF16), wb_ref[...])
    o_ref[...] = (_sigmoid(ga_ref[...]) * a + _sigmoid(gb_ref[...]) * b).astype(o_ref.dtype)


def merge_branches(o_a, o_b, u, w_a, w_b, bm):
    n = o_a.shape[0]
    bn = min(MM_COLS, D_MODEL)
    assert IN_OFFS[7] % bn == 0 and IN_OFFS[8] % bn == 0 and D_MODEL % bn == 0
    ca, cb = IN_OFFS[7] // bn, IN_OFFS[8] // bn
    return pl.pallas_call(
        _merge_body,
        grid=(D_MODEL // bn, n // bm),
        in_specs=[pl.BlockSpec((bm, A_VW), lambda j, i: (i, 0)),
                  pl.BlockSpec((bm, B_W), lambda j, i: (i, 0)),
                  pl.BlockSpec((bm, bn), lambda j, i: (i, ca + j)),
                  pl.BlockSpec((bm, bn), lambda j, i: (i, cb + j)),
                  pl.BlockSpec((A_VW, bn), lambda j, i: (0, j)),
                  pl.BlockSpec((B_W, bn), lambda j, i: (0, j))],
        out_specs=pl.BlockSpec((bm, bn), lambda j, i: (i, j)),
        out_shape=jax.ShapeDtypeStruct((n, D_MODEL), BF16),
        compiler_params=_cparams("parallel", "parallel"),
        name="merge_branches",
    )(o_a, o_b, u, u, w_a, w_b)


def _router_body(xp_ref, xs_ref, nw_ref, wr_ref, br_ref, hp_ref, idx_ref, gate_ref, *, n_prompt_tiles):
    i = pl.program_id(0)
    x1 = jnp.where(i < n_prompt_tiles, xp_ref[...], xs_ref[...])
    h2 = _rms(x1, nw_ref[...])
    for s in range(WORDS_PER_ROW):
        hp_ref[s] = h2[:, s * LANES:(s + 1) * LANES]
    h_hi = h2.astype(BF16)
    h_lo = (h2 - h_hi.astype(F32)).astype(BF16)
    a = _dot(h_hi, wr_ref[...])
    b = _dot(h_lo, wr_ref[...])
    logits = a + pltpu.roll(a, LANES - N_EXPERTS, 1) + b + br_ref[...]
    lane = lax.broadcasted_iota(jnp.int32, logits.shape, 1)
    g = jnp.where(lane < N_EXPERTS, logits, -jnp.inf)
    idx = jnp.zeros(logits.shape, jnp.int32)
    val = jnp.zeros(logits.shape, F32)
    top = None
    for r in range(TOP_K):
        mx = jnp.max(g, axis=-1, keepdims=True)
        ix = jnp.min(jnp.where(g == mx, lane, LANES), axis=-1, keepdims=True)
        top = mx if top is None else top
        idx = jnp.where(lane == r, ix, idx)
        val = jnp.where(lane == r, jnp.exp(mx - top), val)
        g = jnp.where(lane == ix, -jnp.inf, g)
    idx_ref[...] = idx
    gate_ref[...] = val / jnp.sum(val, axis=-1, keepdims=True)


def _dual_maps(tp):
    return (lambda i: (jnp.minimum(i, tp - 1), 0)), (lambda i: (jnp.maximum(i - tp, 0), 0))


def moe_router(x1_p, x1_s, norm_w, w_router, b_router):
    bm = ROW_TILE
    n_p, n_s = x1_p.shape[0], x1_s.shape[0]
    assert n_p % bm == 0 and n_s % bm == 0 and 2 * N_EXPERTS <= LANES and TOP_K <= LANES
    tp, ts = n_p // bm, n_s // bm
    n = n_p + n_s
    w_hi = w_router.astype(BF16)
    w_lo = (w_router - w_hi.astype(F32)).astype(BF16)
    wr = jnp.zeros((D_MODEL, LANES), BF16).at[:, :N_EXPERTS].set(w_hi).at[:, N_EXPERTS:2 * N_EXPERTS].set(w_lo)
    br = jnp.zeros((1, LANES), F32).at[0, :N_EXPERTS].set(b_router)
    p_map, s_map = _dual_maps(tp)
    full = lambda i: (0, 0)
    row = lambda i: (i, 0)
    return pl.pallas_call(
        functools.partial(_router_body, n_prompt_tiles=tp),
        grid=(tp + ts,),
        in_specs=[pl.BlockSpec((bm, D_MODEL), p_map), pl.BlockSpec((bm, D_MODEL), s_map),
                  pl.BlockSpec((1, D_MODEL), full),
                  pl.BlockSpec((D_MODEL, LANES), full), pl.BlockSpec((1, LANES), full)],
        out_specs=(pl.BlockSpec((WORDS_PER_ROW, bm, LANES), lambda i: (0, i, 0)),
                   pl.BlockSpec((bm, LANES), row), pl.BlockSpec((bm, LANES), row)),
        out_shape=(jax.ShapeDtypeStruct((WORDS_PER_ROW, n, LANES), F32),
                   jax.ShapeDtypeStruct((n, LANES), jnp.int32),
                   jax.ShapeDtypeStruct((n, LANES), F32)),
        compiler_params=_cparams("parallel"),
        name="moe_router",
    )(x1_p, x1_s, norm_w.reshape(1, D_MODEL), wr, br)


def _rank_body(idx_ref, rank_ref, cnt_ref, carry_ref):
    i = pl.program_id(0)

    @pl.when(i == 0)
    def _():
        carry_ref[...] = jnp.zeros_like(carry_ref)

    idx = idx_ref[...]
    bt = idx.shape[0]
    lane = lax.broadcasted_iota(jnp.int32, (bt, LANES), 1)
    rowi = lax.broadcasted_iota(jnp.int32, (bt, bt), 0)
    coli = lax.broadcasted_iota(jnp.int32, (bt, bt), 1)
    before = jnp.where(rowi > coli, 1.0, 0.0).astype(BF16)
    base = carry_ref[...]
    res = jnp.zeros((bt, LANES), jnp.int32)
    for k in range(TOP_K):
        onehot = jnp.where(lane == idx[:, k:k + 1], 1.0, 0.0)
        earlier = _dot(before, onehot.astype(BF16))
        rk = jnp.sum(onehot * (earlier + base), axis=-1, keepdims=True)
        res = jnp.where(lane == k, rk.astype(jnp.int32), res)
        base = base + jnp.sum(onehot, axis=0, keepdims=True)
    rank_ref[...] = res
    carry_ref[...] = base
    cnt_ref[...] = jnp.broadcast_to(base, cnt_ref.shape).astype(jnp.int32)


def moe_rank(idx):
    n = idx.shape[0]
    bt = ROW_TILE
    return pl.pallas_call(
        _rank_body,
        grid=(n // bt,),
        in_specs=[pl.BlockSpec((bt, LANES), lambda i: (i, 0))],
        out_specs=(pl.BlockSpec((bt, LANES), lambda i: (i, 0)),
                   pl.BlockSpec((SUBLANES, LANES), lambda i: (0, 0))),
        out_shape=(jax.ShapeDtypeStruct((n, LANES), jnp.int32),
                   jax.ShapeDtypeStruct((SUBLANES, LANES), jnp.int32)),
        scratch_shapes=[pltpu.VMEM((1, LANES), F32)],
        compiler_params=_cparams("arbitrary"),
        name="moe_rank",
    )(idx)


def _load_slots(dest_ref, dsm, sem_idx):
    cp = pltpu.make_async_copy(dest_ref.at[pl.program_id(0)], dsm, sem_idx)
    cp.start()
    cp.wait()


def _dispatch_body(pe_ref, nu_ref, hp_ref, dest_ref, wsrc_ref, xs_ref, wdst_ref, dsm, zbuf, sem_idx, sem, zsem,
                   *, bt, n_blocks):
    wpr = WORDS_PER_ROW
    wdst_ref[...] = wsrc_ref[...].astype(BF16)

    @pl.when(pl.program_id(0) == 0)
    def _():
        zbuf[...] = jnp.zeros_like(zbuf)

        def clear(row0):
            return pltpu.make_async_copy(
                zbuf, xs_ref.at[pl.ds(pl.multiple_of(row0 * wpr, wpr), MOE_ROWS * wpr), :], zsem)

        def seg_tail(e):
            return jnp.maximum(pe_ref[e] - MOE_ROWS, 0)

        for e in range(N_EXPERTS):
            clear(seg_tail(e)).start()
        lax.fori_loop(nu_ref[0], n_blocks, lambda b, c: (clear(b * MOE_ROWS).start(), c)[1], 0)
        for e in range(N_EXPERTS):
            clear(seg_tail(e)).wait()
        lax.fori_loop(nu_ref[0], n_blocks, lambda b, c: (clear(b * MOE_ROWS).wait(), c)[1], 0)

    _load_slots(dest_ref, dsm, sem_idx)

    def issue(r, c):
        for k in range(TOP_K):
            d = dsm[r * TOP_K + k]
            pltpu.make_async_copy(hp_ref.at[:, r, :],
                                  xs_ref.at[pl.ds(pl.multiple_of(d * wpr, wpr), wpr), :],
                                  sem).start(priority=k % 2)
        return c

    lax.fori_loop(0, bt, issue, 0, unroll=DMA_UNROLL)
    for k in range(TOP_K):
        whole = xs_ref.at[pl.ds(0, bt * wpr), :]
        pltpu.make_async_copy(whole, whole, sem).wait()


def moe_dispatch(hp, dest, pends, n_used, n_blocks, cast_src):
    bt = ROW_TILE
    nt = dest.shape[0]
    wpr = WORDS_PER_ROW
    ns = cast_src.shape[0]
    assert ns <= nt
    slice_spec = pl.BlockSpec((None,) + cast_src.shape[1:], lambda i, pe, nu: (jnp.minimum(i, ns - 1), 0, 0))
    grid_spec = pltpu.PrefetchScalarGridSpec(
        num_scalar_prefetch=2,
        grid=(nt,),
        in_specs=[pl.BlockSpec((wpr, bt, LANES), lambda i, pe, nu: (0, i, 0)),
                  pl.BlockSpec(memory_space=pl.ANY), slice_spec],
        out_specs=(pl.BlockSpec(memory_space=pl.ANY), slice_spec),
        scratch_shapes=[pltpu.SMEM((bt * TOP_K,), jnp.int32),
                        pltpu.VMEM((MOE_ROWS * wpr, LANES), F32),
                        pltpu.SemaphoreType.DMA, pltpu.SemaphoreType.DMA, pltpu.SemaphoreType.DMA],
    )
    return pl.pallas_call(
        functools.partial(_dispatch_body, bt=bt, n_blocks=n_blocks),
        grid_spec=grid_spec,
        out_shape=(jax.ShapeDtypeStruct((n_blocks * MOE_ROWS * wpr, LANES), F32),
                   jax.ShapeDtypeStruct(cast_src.shape, BF16)),
        compiler_params=_cparams("arbitrary"),
        name="moe_dispatch",
    )(pends, n_used, hp, dest, cast_src)


def _experts_body(be_ref, nu_ref, x_ref, w1g_ref, w1l_ref, b1g_ref, b1l_ref, w2_ref, b2_ref,
                  o_ref, xb_ref, acc_ref, *, rows):
    i = pl.program_id(0)
    c = pl.program_id(1)
    wpr = WORDS_PER_ROW

    @pl.when(i < nu_ref[0])
    def _():
        @pl.when(c == 0)
        def _():
            for s in range(wpr):
                xb_ref[:, s * LANES:(s + 1) * LANES] = x_ref[pl.ds(s, rows, stride=wpr), :].astype(BF16)
            acc_ref[...] = jnp.broadcast_to(b2_ref[...], acc_ref.shape)

        xb = xb_ref[...]
        gu = _dot(xb, w1g_ref[...]) + b1g_ref[...]
        li = _dot(xb, w1l_ref[...]) + b1l_ref[...]
        glu = jnp.minimum(gu, SWIGLU_LIMIT)
        lin = jnp.clip(li, -SWIGLU_LIMIT, SWIGLU_LIMIT)
        act = glu * _sigmoid(SWIGLU_ALPHA * glu) * (lin + 1.0)
        acc_ref[...] += _dot(act.astype(BF16), w2_ref[...])

        @pl.when(c == pl.num_programs(1) - 1)
        def _():
            acc = acc_ref[...]
            for s in range(wpr):
                o_ref[pl.ds(s, rows, stride=wpr), :] = acc[:, s * LANES:(s + 1) * LANES]

    @pl.when((i >= nu_ref[0]) & (c == 0))
    def _():
        o_ref[...] = jnp.zeros_like(o_ref)


def moe_experts(xs, blk_exp, n_used, w1, b1, w2, b2):
    rows, ff = MOE_ROWS, min(MOE_FF, D_FF)
    wpr = WORDS_PER_ROW
    n_rows = xs.shape[0] // wpr
    nb = n_rows // rows
    nc = D_FF // ff
    lin_off = D_FF // ff

    def eff(i, c, be, nu):
        live = i < nu[0]
        return jnp.where(live, i, nu[0] - 1), jnp.where(live, c, nc - 1)

    def x_map(i, c, be, nu):
        return (eff(i, c, be, nu)[0], 0)

    def w1g_map(i, c, be, nu):
        ii, cc = eff(i, c, be, nu)
        return (be[ii], 0, cc)

    def w1l_map(i, c, be, nu):
        ii, cc = eff(i, c, be, nu)
        return (be[ii], 0, lin_off + cc)

    def w2_map(i, c, be, nu):
        ii, cc = eff(i, c, be, nu)
        return (be[ii], cc, 0)

    def b2_map(i, c, be, nu):
        return (be[eff(i, c, be, nu)[0]], 0, 0)

    grid_spec = pltpu.PrefetchScalarGridSpec(
        num_scalar_prefetch=2,
        grid=(nb, nc),
        in_specs=[pl.BlockSpec((rows * wpr, LANES), x_map),
                  pl.BlockSpec((None, D_MODEL, ff), w1g_map),
                  pl.BlockSpec((None, D_MODEL, ff), w1l_map),
                  pl.BlockSpec((None, 1, ff), w1g_map),
                  pl.BlockSpec((None, 1, ff), w1l_map),
                  pl.BlockSpec((None, ff, D_MODEL), w2_map),
                  pl.BlockSpec((None, 1, D_MODEL), b2_map)],
        out_specs=pl.BlockSpec((rows * wpr, LANES), lambda i, c, be, nu: (i, 0)),
        scratch_shapes=[pltpu.VMEM((rows, D_MODEL), BF16), pltpu.VMEM((rows, D_MODEL), F32)],
    )
    return pl.pallas_call(
        functools.partial(_experts_body, rows=rows),
        grid_spec=grid_spec,
        out_shape=jax.ShapeDtypeStruct((n_rows * wpr, LANES), F32),
        compiler_params=_cparams("arbitrary", "arbitrary"),
        name="moe_experts",
    )(blk_exp, n_used, xs, w1, w1, b1, b1, w2, b2)


def _tail_body(xp_ref, xs_ref, gate_ref, pp_ref, ps_ref, nw_ref, wg_ref, wp_ref, nf_ref, dest_ref, y_ref,
               op_ref, os_ref, buf, dsm, sem_idx, sems, *, bt, n_prompt_tiles, n_tiles):
    i = pl.program_id(0)
    slot = i % 2
    wpr = WORDS_PER_ROW
    nslots = bt * TOP_K

    def fetch(tile, sl):
        cp = pltpu.make_async_copy(dest_ref.at[tile], dsm.at[pl.ds(sl * nslots, nslots)], sem_idx)
        cp.start()
        cp.wait()

        def issue(r, c):
            for k in range(TOP_K):
                d = dsm[sl * nslots + r * TOP_K + k]
                pltpu.make_async_copy(y_ref.at[pl.ds(pl.multiple_of(d * wpr, wpr), wpr), :],
                                      buf.at[sl, k, :, r, :], sems.at[sl]).start(priority=k % 2)
            return c

        lax.fori_loop(0, bt, issue, 0, unroll=DMA_UNROLL)

    @pl.when(i == 0)
    def _():
        fetch(0, 0)

    @pl.when(i + 1 < n_tiles)
    def _():
        fetch(i + 1, 1 - slot)

    for k in range(TOP_K):
        whole = y_ref.at[pl.ds(0, bt * wpr), :]
        pltpu.make_async_copy(whole, whole, sems.at[slot]).wait()

    first = i < n_prompt_tiles
    gates = gate_ref[...]
    gk = [jnp.broadcast_to(gates[:, k:k + 1], (bt, LANES)) for k in range(TOP_K)]
    cols = []
    for s in range(wpr):
        acc = None
        for k in range(TOP_K):
            term = gk[k] * buf[slot, k, s]
            acc = term if acc is None else acc + term
        cols.append(acc)
    x = jnp.where(first, xp_ref[...], xs_ref[...]) + jnp.concatenate(cols, axis=1)
    p = jnp.where(first, pp_ref[...], ps_ref[...])
    gate = _sigmoid(_dot(_rms(x, nw_ref[...]).astype(BF16), wg_ref[...]))
    x = x + gate * _dot(p.astype(BF16), wp_ref[...])
    out = _rms(x, nf_ref[...])

    @pl.when(first)
    def _():
        op_ref[...] = out

    @pl.when(jnp.logical_not(first))
    def _():
        os_ref[...] = out


def moe_combine_final(x1_p, x1_s, gates, dest, y, p_p, p_s, norm_ple, w_gate, w_ple, norm_final):
    bt = ROW_TILE
    n_p, n_s = x1_p.shape[0], x1_s.shape[0]
    tp, ts = n_p // bt, n_s // bt
    p_map, s_map = _dual_maps(tp)
    full = lambda i: (0, 0)
    return pl.pallas_call(
        functools.partial(_tail_body, bt=bt, n_prompt_tiles=tp, n_tiles=tp + ts),
        grid=(tp + ts,),
        in_specs=[pl.BlockSpec((bt, D_MODEL), p_map), pl.BlockSpec((bt, D_MODEL), s_map),
                  pl.BlockSpec((bt, LANES), lambda i: (i, 0)),
                  pl.BlockSpec((bt, PLE_DIM), p_map), pl.BlockSpec((bt, PLE_DIM), s_map),
                  pl.BlockSpec((1, D_MODEL), full), pl.BlockSpec((D_MODEL, D_MODEL), full),
                  pl.BlockSpec((PLE_DIM, D_MODEL), full), pl.BlockSpec((1, D_MODEL), full),
                  pl.BlockSpec(memory_space=pl.ANY),
                  pl.BlockSpec(memory_space=pl.ANY)],
        out_specs=(pl.BlockSpec((bt, D_MODEL), p_map), pl.BlockSpec((bt, D_MODEL), s_map)),
        out_shape=(jax.ShapeDtypeStruct((n_p, D_MODEL), F32), jax.ShapeDtypeStruct((n_s, D_MODEL), F32)),
        scratch_shapes=[pltpu.VMEM((2, TOP_K, WORDS_PER_ROW, bt, LANES), F32),
                        pltpu.SMEM((2 * bt * TOP_K,), jnp.int32),
                        pltpu.SemaphoreType.DMA, pltpu.SemaphoreType.DMA((2,))],
        compiler_params=_cparams("arbitrary"),
        name="moe_combine_final",
    )(x1_p, x1_s, gates, p_p, p_s, norm_ple.reshape(1, D_MODEL), w_gate, w_ple,
      norm_final.reshape(1, D_MODEL), dest, y)


def _moe_ffn_tail(x1_p, x1_s, norm_w, w_router, b_router, w1, b1, w2, b2, tail_args):
    hp, idx, gates = moe_router(x1_p, x1_s, norm_w, w_router, b_router)
    n = idx.shape[0]
    bt = ROW_TILE
    rank, counts = moe_rank(idx)
    counts = counts[0, :N_EXPERTS]
    pcounts = (counts + MOE_ROWS - 1) // MOE_ROWS * MOE_ROWS
    pends = jnp.cumsum(pcounts)
    pstarts = pends - pcounts
    e_tok = idx[:, :TOP_K]
    dest = (pstarts[e_tok] + rank[:, :TOP_K]).astype(jnp.int32).reshape(n // bt, bt * TOP_K)
    n_blocks = -(-(n * TOP_K + N_EXPERTS * (MOE_ROWS - 1)) // MOE_ROWS)
    blk_start = jnp.arange(n_blocks, dtype=jnp.int32) * MOE_ROWS
    blk_exp = jnp.minimum(jnp.sum((pends[None, :] <= blk_start[:, None]).astype(jnp.int32), axis=1),
                          N_EXPERTS - 1).astype(jnp.int32)
    n_used = (pends[-1] // MOE_ROWS).astype(jnp.int32).reshape(1)
    ns = 1 << ((n // bt).bit_length() - 1)
    w2_rows = w2.shape[0] * w2.shape[1]
    assert w2_rows % (ns * 2 * SUBLANES) == 0
    xs, w2_b = moe_dispatch(hp, dest, pends.astype(jnp.int32), n_used, n_blocks,
                            w2.reshape(ns, w2_rows // ns, w2.shape[2]))
    y = moe_experts(xs, blk_exp, n_used, w1, b1, w2_b.reshape(w2.shape), b2)
    return moe_combine_final(x1_p, x1_s, gates, dest, y, *tail_args)


def _row_tile(n, pref):
    return pref if n % pref == 0 else ROW_TILE


def kernel(x_prompt, x_sample, cache_k, cache_v, state_hgrn, page_table, p_prompt, p_sample, hgrn_lb, norm_mix, w_in, hgrn_norm, w_a, w_b, w_o, norm_moe, w_router, b_router, w_moe1, b_moe1, w_moe2, b_moe2, norm_ple, w_ple, w_ple_gate, norm_final):
    assert DEPTH == 1
    bsz, seq, _ = x_prompt.shape
    dbsz, tq, _ = x_sample.shape
    n_p, n_s = bsz * seq, dbsz * tq
    l = 0
    lb = jnp.cumsum(jax.nn.softmax(hgrn_lb.astype(F32), axis=0), axis=0)[l]
    w_in_b = w_in[l].astype(BF16)
    w_a_b, w_b_b, w_o_b = w_a[l].astype(BF16), w_b[l].astype(BF16), w_o[l].astype(BF16)
    w1_f, w2_f = w_moe1[l], w_moe2[l]
    b1 = b_moe1[l].reshape(N_EXPERTS, 1, 2 * D_FF)
    b2 = b_moe2[l].reshape(N_EXPERTS, 1, D_MODEL)
    wg_b, wp_b = w_ple_gate[l].astype(BF16), w_ple[l].astype(BF16)
    bn = MM_COLS if IN_DIM % MM_COLS == 0 and D_MODEL % MM_COLS == 0 else B_W

    xp = x_prompt.reshape(n_p, D_MODEL)
    xs = x_sample.reshape(n_s, D_MODEL)

    def mixer_inputs(x, n):
        h = rmsnorm_cast(x, norm_mix[l], _row_tile(n, 512))
        return matmul(h, w_in_b, _row_tile(n, MM_ROWS), bn)

    u_p = mixer_inputs(xp, n_p)
    hg_steps = bsz * (A_HEADS // min(HGRN_HEADS, A_HEADS)) * (seq // min(seq, HGRN_CHUNK))
    w1_rows = N_EXPERTS * D_MODEL
    assert w1_rows % (hg_steps * 2 * SUBLANES) == 0
    oa_p, st_p, w1_b = hgrn2(u_p, lb, hgrn_norm[l], jnp.zeros((bsz, A_HEADS, A_KDIM, A_VDIM), F32), bsz, seq, BF16,
                             cast_src=w1_f.reshape(hg_steps, w1_rows // hg_steps, 2 * D_FF))
    w1_b = w1_b.reshape(N_EXPERTS, D_MODEL, 2 * D_FF)
    rt = _row_tile(seq, 512)
    q_p, k_p, v_p = rope_qkv(u_p, _rope_tables(jnp.arange(seq)), rt if seq % rt == 0 else seq)
    ob_p = moba_prompt(q_p, k_p, v_p, bsz, seq)
    m_p = merge_branches(oa_p, ob_p, u_p, w_a_b, w_b_b, _row_tile(n_p, MM_ROWS))
    x1_p = matmul(m_p, w_o_b, _row_tile(n_p, MM_ROWS), bn, residual=xp)

    u_s = mixer_inputs(xs, n_s)
    oa_s, st_s = hgrn2(u_s, lb, hgrn_norm[l], state_hgrn[l], dbsz, tq, F32)
    tabs_s = tuple(jnp.tile(t, (dbsz, 1)) for t in _rope_tables(PAST_LEN + jnp.arange(tq)))
    q_s, k_s, v_s = rope_qkv(u_s, tabs_s, n_s)
    k_sel = min(MOBA_TOPK, PAST_LEN // MOBA_BLOCK)
    sel = moba_select(q_s, cache_k[l], page_table, tq)[:, :, :k_sel].reshape(-1)
    ob_s = moba_decode(q_s, k_s, v_s, cache_k[l], cache_v[l], page_table, sel, tq)
    m_s = merge_branches(oa_s, ob_s, u_s, w_a_b, w_b_b, _row_tile(n_s, MM_ROWS))
    x1_s = matmul(m_s, w_o_b, _row_tile(n_s, MM_ROWS), bn, residual=xs)

    tail_args = (p_prompt[l].reshape(n_p, PLE_DIM), p_sample[l].reshape(n_s, PLE_DIM),
                 norm_ple[l], wg_b, wp_b, norm_final)
    y_p, y_s = _moe_ffn_tail(x1_p, x1_s, norm_moe[l], w_router[l], b_router[l], w1_b, b1, w2_f, b2, tail_args)

    hs_p = (1, bsz, seq, B_HEADS, B_HDIM)
    hs_s = (1, dbsz, tq, B_HEADS, B_HDIM)
    return (y_p.reshape(bsz, seq, D_MODEL), y_s.reshape(dbsz, tq, D_MODEL),
            st_p[None], k_p.reshape(hs_p), v_p.reshape(hs_p),
            st_s[None], k_s.reshape(hs_s), v_s.reshape(hs_s))
```
